```python
import math
import jax, jax.numpy as jnp
from jax import lax
import numpy as np

D_MODEL = 1024
BATCH = 8
SEQ = 2048
DEPTH = 1
DEC_BATCH = 4
DEC_SEQ = 4096
PAST_LEN = 128

N_META = 16
GRID_W = 64
DIFF_HEADS = 8
DIFF_DH = 64
DIFF_VD = 2 * DIFF_DH
NA_HEADS = 16
NA_DH = 64
NA_WIN_H = 8
NA_WIN_W = 16
ROT_DIMS = DIFF_DH // 4
ROPE_THETA = 500000.0
Q_BLOCK = 128
N_EXPERTS = 32
TOP_K = 4
D_FF = D_MODEL
SWIGLU_ALPHA = 1.702
SWIGLU_LIMIT = 7.0
MOE_BLOCK = 128
EPS = 1e-6
NEG = -1e30

DIFF_QK_W = DIFF_HEADS * 2 * DIFF_DH
DIFF_V_W = DIFF_HEADS * DIFF_VD
NA_W = NA_HEADS * NA_DH
IN_WIDTHS = [DIFF_QK_W, DIFF_QK_W, DIFF_V_W, NA_W, NA_W, NA_W, D_MODEL, D_MODEL]
IN_SPLITS = [int(s) for s in np.cumsum(IN_WIDTHS)[:-1]]
W_IN_COLS = int(sum(IN_WIDTHS))

kernel_name = "hybrid_diffattn_natten_moe_encoder"


def rms_norm(x, g):
    xf = x.astype(jnp.float32)
    y = xf * lax.rsqrt(jnp.mean(xf * xf, axis=-1, keepdims=True) + EPS)
    return (y * g.astype(jnp.float32)).astype(x.dtype)


def rope_tables(n):
    inv = ROPE_THETA ** (-jnp.arange(0, ROT_DIMS, 2, dtype=jnp.float32) / ROT_DIMS)
    ang = jnp.arange(n, dtype=jnp.float32)[:, None] * inv[None, :]
    return jnp.cos(ang), jnp.sin(ang)


def apply_partial_rope(x, cos, sin):
    half = ROT_DIMS // 2
    xr = x[..., :ROT_DIMS].astype(jnp.float32)
    x1, x2 = xr[..., :half], xr[..., half:]
    c = cos[None, :, None, None, :]
    s = sin[None, :, None, None, :]
    rot = jnp.concatenate([x1 * c - x2 * s, x2 * c + x1 * s], axis=-1)
    return jnp.concatenate([rot.astype(x.dtype), x[..., ROT_DIMS:]], axis=-1)


def diff_attn_rows(q_rows, k, v, lam):
    s = jnp.einsum('bqhcd,bkhcd->bhcqk', q_rows, k).astype(jnp.float32) * (DIFF_DH ** -0.5)
    p = jax.nn.softmax(s, axis=-1)
    a = p[:, :, 0] - lam * p[:, :, 1]
    return jnp.einsum('bhqk,bkhe->bqhe', a.astype(v.dtype), v)


def differential_attention(q, k, v, lam, lam_init, sub_g):
    B, L = q.shape[0], q.shape[1]
    n = L - N_META
    o_meta = diff_attn_rows(q[:, :N_META], k, v, lam)
    qb = q[:, N_META:].reshape(B, n // Q_BLOCK, Q_BLOCK, DIFF_HEADS, 2, DIFF_DH)
    qb = jnp.moveaxis(qb, 1, 0)
    o_real = lax.map(lambda qq: diff_attn_rows(qq, k, v, lam), qb)
    o_real = jnp.moveaxis(o_real, 0, 1).reshape(B, n, DIFF_HEADS, DIFF_VD)
    o = jnp.concatenate([o_meta, o_real], axis=1)
    o = rms_norm(o, sub_g) * (1.0 - lam_init)
    return o.reshape(B, L, DIFF_V_W)


def neighbourhood_attention(q, k, v, rpb, meta_bias):
    B, L, H, d = q.shape
    n = L - N_META
    rows = n // GRID_W
    wh = min(NA_WIN_H, rows)
    scale = NA_DH ** -0.5
    qm, km, vm = q[:, :N_META], k[:, :N_META], v[:, :N_META]
    kg = k[:, N_META:].reshape(B, rows, GRID_W, H, d)
    vg = v[:, N_META:].reshape(B, rows, GRID_W, H, d)
    qg = jnp.moveaxis(q[:, N_META:].reshape(B, rows, GRID_W, H, d), 1, 0)

    mb = meta_bias.astype(jnp.float32)[None, :, None, :]
    sm = jnp.einsum('bqhd,bmhd->bhqm', qm, km).astype(jnp.float32) * scale + mb
    om = jnp.einsum('bhqm,bmhd->bqhd', jax.nn.softmax(sm, axis=-1).astype(v.dtype), vm)

    col = np.arange(GRID_W)
    col_start = np.clip(col - NA_WIN_W // 2, 0, GRID_W - NA_WIN_W)
    col_mask = (col[None, :] >= col_start[:, None]) & (col[None, :] < col_start[:, None] + NA_WIN_W)
    dc_idx = np.clip(col[None, :] - col[:, None], -(NA_WIN_W - 1), NA_WIN_W - 1) + NA_WIN_W - 1
    rpb_c = rpb.astype(jnp.float32)[:, :, dc_idx]
    r = np.arange(rows)
    starts = np.clip(r - wh // 2, 0, rows - wh)
    dr_idx = starts[:, None] + np.arange(wh)[None, :] - r[:, None] + NA_WIN_H - 1

    def row_fn(args):
        qr, start, dri = args
        kb = lax.dynamic_slice_in_dim(kg, start, wh, axis=1)
        vb = lax.dynamic_slice_in_dim(vg, start, wh, axis=1)
        bias = jnp.transpose(rpb_c[:, dri], (0, 2, 1, 3))
        s = jnp.einsum('bqhd,bjkhd->bhqjk', qr, kb).astype(jnp.float32) * scale + bias[None]
        s = jnp.where(col_mask[None, None, :, None, :], s, NEG)
        s_meta = jnp.einsum('bqhd,bmhd->bhqm', qr, km).astype(jnp.float32) * scale + mb
        s_all = jnp.concatenate([s_meta, s.reshape(B, H, GRID_W, wh * GRID_W)], axis=-1)
        p = jax.nn.softmax(s_all, axis=-1).astype(v.dtype)
        pm = p[..., :N_META]
        pg = p[..., N_META:].reshape(B, H, GRID_W, wh, GRID_W)
        return (jnp.einsum('bhqm,bmhd->bqhd', pm, vm)
                + jnp.einsum('bhqjk,bjkhd->bqhd', pg, vb))

    og = lax.map(row_fn, (qg, jnp.asarray(starts, jnp.int32), jnp.asarray(dr_idx, jnp.int32)))
    og = jnp.moveaxis(og, 0, 1).reshape(B, n, H, d)
    return jnp.concatenate([om, og], axis=1).reshape(B, L, NA_W)


def moe_ffn(h, w_router, b_router, w_gu, b_gu, w_down, b_down):
    B, L, D = h.shape
    x = h.reshape(-1, D)
    T = x.shape[0]
    logits = (x @ w_router + b_router).astype(jnp.float32)
    top_v, top_i = lax.top_k(logits, TOP_K)
    gates = jax.nn.softmax(top_v, axis=-1)
    A = T * TOP_K
    flat_e = top_i.reshape(-1).astype(jnp.int32)
    flat_tok = jnp.arange(A, dtype=jnp.int32) // TOP_K
    flat_g = gates.reshape(-1)
    order = jnp.argsort(flat_e)
    se = flat_e[order]
    counts = jax.ops.segment_sum(jnp.ones((A,), jnp.int32), flat_e, num_segments=N_EXPERTS)
    padded = (counts + MOE_BLOCK - 1) // MOE_BLOCK * MOE_BLOCK
    pend = jnp.cumsum(padded)
    pstart = pend - padded
    ustart = jnp.cumsum(counts) - counts
    dest = pstart[se] + jnp.arange(A, dtype=jnp.int32) - ustart[se]
    n_blocks = -(-(A + N_EXPERTS * (MOE_BLOCK - 1)) // MOE_BLOCK)
    P = n_blocks * MOE_BLOCK
    slot_tok = jnp.zeros((P,), jnp.int32).at[dest].set(flat_tok[order])
    slot_g = jnp.zeros((P,), jnp.float32).at[dest].set(flat_g[order])
    block_e = jnp.minimum(
        jnp.searchsorted(pend, jnp.arange(n_blocks, dtype=jnp.int32) * MOE_BLOCK, side='right'),
        N_EXPERTS - 1).astype(jnp.int32)
    xs = x[slot_tok].reshape(n_blocks, MOE_BLOCK, D)

    def expert_rows(args):
        xb, e = args
        gu = xb @ w_gu[e] + b_gu[e]
        gate = jnp.minimum(gu[:, :D_FF], SWIGLU_LIMIT)
        up = jnp.clip(gu[:, D_FF:], -SWIGLU_LIMIT, SWIGLU_LIMIT)
        act = gate * jax.nn.sigmoid(SWIGLU_ALPHA * gate) * (up + 1.0)
        return act @ w_down[e] + b_down[e]

    yb = lax.map(expert_rows, (xs, block_e)).reshape(P, D)
    y = jax.ops.segment_sum(yb * slot_g[:, None].astype(yb.dtype), slot_tok, num_segments=T)
    return y.reshape(B, L, D)


def encoder_layer(x, l, cos, sin, norm1_g, w_in, diff_q_g, diff_k_g, lam_q1, lam_k1, lam_q2, lam_k2,
                  diff_sub_g, w_diff_out, na_q_g, na_k_g, na_rpb, na_meta_bias, w_na_out, w_o,
                  norm2_g, w_router, b_router, w_gate_up, b_gate_up, w_down, b_down):
    B, L, D = x.shape
    h = rms_norm(x, norm1_g)
    proj = h @ w_in
    qd, kd, vd, qn, kn, vn, ga, gb = jnp.split(proj, IN_SPLITS, axis=-1)

    qd = apply_partial_rope(rms_norm(qd.reshape(B, L, DIFF_HEADS, 2, DIFF_DH), diff_q_g), cos, sin)
    kd = apply_partial_rope(rms_norm(kd.reshape(B, L, DIFF_HEADS, 2, DIFF_DH), diff_k_g), cos, sin)
    vd = vd.reshape(B, L, DIFF_HEADS, DIFF_VD)
    lam_init = 0.8 - 0.6 * math.exp(-0.3 * l)
    lam = (jnp.exp(jnp.sum(lam_q1.astype(jnp.float32) * lam_k1.astype(jnp.float32)))
           - jnp.exp(jnp.sum(lam_q2.astype(jnp.float32) * lam_k2.astype(jnp.float32))) + lam_init)
    o_a = differential_attention(qd, kd, vd, lam, lam_init, diff_sub_g)

    qn = rms_norm(qn.reshape(B, L, NA_HEADS, NA_DH), na_q_g)
    kn = rms_norm(kn.reshape(B, L, NA_HEADS, NA_DH), na_k_g)
    vn = vn.reshape(B, L, NA_HEADS, NA_DH)
    o_b = neighbourhood_attention(qn, kn, vn, na_rpb, na_meta_bias)

    merged = jax.nn.sigmoid(ga) * (o_a @ w_diff_out) + jax.nn.sigmoid(gb) * (o_b @ w_na_out)
    x = x + merged @ w_o

    x = x + moe_ffn(rms_norm(x, norm2_g), w_router, b_router, w_gate_up, b_gate_up, w_down, b_down)
    return x


def setup_inputs(seed: int = 0) -> dict:
    key = jax.random.key(seed)
    ks = jax.random.split(key, 32)
    f32 = jnp.float32

    def nrm(k, shape, scale):
        return jax.random.normal(k, shape, f32) * scale

    def gain(k, shape):
        return 1.0 + 0.01 * jax.random.normal(k, shape, f32)

    return {
        "x_prompt": nrm(ks[0], (BATCH, SEQ, D_MODEL), 1.0),
        "x_sample": nrm(ks[1], (DEC_BATCH, DEC_SEQ, D_MODEL), 1.0),
        "meta_tokens": nrm(ks[2], (N_META, D_MODEL), 1.0),
        "norm1_g": gain(ks[3], (DEPTH, D_MODEL)),
        "w_in": nrm(ks[4], (DEPTH, D_MODEL, W_IN_COLS), D_MODEL ** -0.5),
        "diff_q_g": gain(ks[5], (DEPTH, DIFF_DH)),
        "diff_k_g": gain(ks[6], (DEPTH, DIFF_DH)),
        "lam_q1": nrm(ks[7], (DEPTH, DIFF_DH), 0.1),
        "lam_k1": nrm(ks[8], (DEPTH, DIFF_DH), 0.1),
        "lam_q2": nrm(ks[9], (DEPTH, DIFF_DH), 0.1),
        "lam_k2": nrm(ks[10], (DEPTH, DIFF_DH), 0.1),
        "diff_sub_g": gain(ks[11], (DEPTH, DIFF_VD)),
        "w_diff_out": nrm(ks[12], (DEPTH, DIFF_V_W, D_MODEL), DIFF_V_W ** -0.5),
        "na_q_g": gain(ks[13], (DEPTH, NA_DH)),
        "na_k_g": gain(ks[14], (DEPTH, NA_DH)),
        "na_rpb": nrm(ks[15], (DEPTH, NA_HEADS, 2 * NA_WIN_H - 1, 2 * NA_WIN_W - 1), 0.1),
        "na_meta_bias": nrm(ks[16], (DEPTH, NA_HEADS, N_META), 0.1),
        "w_na_out": nrm(ks[17], (DEPTH, NA_W, D_MODEL), NA_W ** -0.5),
        "w_o": nrm(ks[18], (DEPTH, D_MODEL, D_MODEL), D_MODEL ** -0.5),
        "norm2_g": gain(ks[19], (DEPTH, D_MODEL)),
        "w_router": nrm(ks[20], (DEPTH, D_MODEL, N_EXPERTS), D_MODEL ** -0.5),
        "b_router": nrm(ks[21], (DEPTH, N_EXPERTS), 0.01),
        "w_gate_up": nrm(ks[22], (DEPTH, N_EXPERTS, D_MODEL, 2 * D_FF), D_MODEL ** -0.5),
        "b_gate_up": nrm(ks[23], (DEPTH, N_EXPERTS, 2 * D_FF), 0.01),
        "w_down": nrm(ks[24], (DEPTH, N_EXPERTS, D_FF, D_MODEL), D_FF ** -0.5),
        "b_down": nrm(ks[25], (DEPTH, N_EXPERTS, D_MODEL), 0.01),
    }


def reference(x_prompt, x_sample, meta_tokens, norm1_g, w_in, diff_q_g, diff_k_g, lam_q1, lam_k1,
              lam_q2, lam_k2, diff_sub_g, w_diff_out, na_q_g, na_k_g, na_rpb, na_meta_bias, w_na_out,
              w_o, norm2_g, w_router, b_router, w_gate_up, b_gate_up, w_down, b_down):
    def trunk(x):
        B = x.shape[0]
        meta = jnp.broadcast_to(meta_tokens.astype(x.dtype)[None], (B, N_META, D_MODEL))
        h = jnp.concatenate([meta, x], axis=1)
        cos, sin = rope_tables(h.shape[1])
        for l in range(DEPTH):
            h = encoder_layer(h, l, cos, sin, norm1_g[l], w_in[l], diff_q_g[l], diff_k_g[l],
                              lam_q1[l], lam_k1[l], lam_q2[l], lam_k2[l], diff_sub_g[l], w_diff_out[l],
                              na_q_g[l], na_k_g[l], na_rpb[l], na_meta_bias[l], w_na_out[l], w_o[l],
                              norm2_g[l], w_router[l], b_router[l], w_gate_up[l], b_gate_up[l],
                              w_down[l], b_down[l])
        return h[:, N_META:]

    y_prompt = trunk(x_prompt)
    y_sample = trunk(x_sample)
    return (y_prompt, y_sample)
```

```python
import functools
import math

import numpy as np
import jax
import jax.numpy as jnp
from jax import lax
from jax.experimental import pallas as pl
from jax.experimental.pallas import tpu as pltpu

D_MODEL = 1024
N_META = 16
GRID_W = 64
DIFF_HEADS = 8
DIFF_DH = 64
DIFF_VD = 2 * DIFF_DH
NA_HEADS = 16
NA_DH = 64
NA_WIN_H = 8
NA_WIN_W = 16
ROT_DIMS = DIFF_DH // 4
ROPE_THETA = 500000.0
N_EXPERTS = 32
TOP_K = 4
D_FF = D_MODEL
SWIGLU_ALPHA = 1.702
SWIGLU_LIMIT = 7.0
EPS = 1e-6
NEG = -1e30
LAM_INIT = 0.8 - 0.6 * math.exp(-0.3 * 0)
LOG2E = 1.4426950408889634
Q_SCALE = (DIFF_DH ** -0.5) * LOG2E

N_SEG = 8
LANE = 128
NA_ROWS_PER_STEP = 2
NA_KEY_ROWS = NA_WIN_H + NA_ROWS_PER_STEP
VMEM_LIMIT = 56 * 1024 * 1024

F32 = jnp.float32
BF16 = jnp.bfloat16


def _cparams(sem):
    return pltpu.CompilerParams(dimension_semantics=sem, vmem_limit_bytes=VMEM_LIMIT)


def _head_norm_t(y_t, g_ref, cos, sin, scale):
    tm = y_t.shape[1]
    y3 = y_t.reshape(D_MODEL // DIFF_DH, DIFF_DH, tm)
    ms = jnp.mean(y3 * y3, axis=1, keepdims=True)
    y3 = y3 * lax.rsqrt(ms + EPS) * g_ref[...].reshape(1, DIFF_DH, 1)
    if cos is not None:
        half = ROT_DIMS // 2
        x1 = y3[:, 0:half, :]
        x2 = y3[:, half:ROT_DIMS, :]
        c = cos[None]
        s = sin[None]
        y3 = jnp.concatenate([x1 * c - x2 * s, x2 * c + x1 * s, y3[:, ROT_DIMS:, :]], axis=1)
    if scale != 1.0:
        y3 = y3 * scale
    return y3.reshape(D_MODEL, tm)


def _in_proj_kernel(x_ref, g1_ref, w_ref, cos_ref, sin_ref, gqd_ref, gkd_ref, gqn_ref, gkn_ref,
                    qd_t_ref, kd_ref, vd_t_ref, qn_t_ref, kn_ref, vn_t_ref, ga_ref, gb_ref, h_ref):
    j = pl.program_id(1)

    @pl.when(j == 0)
    def _():
        x = x_ref[...]
        ms = jnp.mean(x * x, axis=-1, keepdims=True)
        h_ref[...] = (x * lax.rsqrt(ms + EPS) * g1_ref[...]).astype(BF16)

    y = jnp.dot(h_ref[...], w_ref[...], preferred_element_type=F32)

    @pl.when(j == 0)
    def _():
        qd_t_ref[...] = _head_norm_t(y.T, gqd_ref, cos_ref[...], sin_ref[...], Q_SCALE).astype(BF16)

    @pl.when(j == 1)
    def _():
        kd_ref[...] = _head_norm_t(y.T, gkd_ref, cos_ref[...], sin_ref[...], 1.0).T.astype(BF16)

    @pl.when(j == 2)
    def _():
        vd_t_ref[...] = y.T.astype(BF16)

    @pl.when(j == 3)
    def _():
        qn_t_ref[...] = _head_norm_t(y.T, gqn_ref, None, None, Q_SCALE).astype(BF16)

    @pl.when(j == 4)
    def _():
        kn_ref[...] = _head_norm_t(y.T, gkn_ref, None, None, 1.0).T.astype(BF16)

    @pl.when(j == 5)
    def _():
        vn_t_ref[...] = y.T.astype(BF16)

    @pl.when(j == 6)
    def _():
        ga_ref[...] = jax.nn.sigmoid(y).astype(BF16)

    @pl.when(j == 7)
    def _():
        gb_ref[...] = jax.nn.sigmoid(y).astype(BF16)


def _in_proj(x2d, g1, w_in, cos_t, sin_t, gqd, gkd, gqn, gkn, tm):
    t = x2d.shape[0]
    n_pos_blocks = cos_t.shape[1] // tm
    tok_major = pl.BlockSpec((tm, D_MODEL), lambda i, j: (i, 0))
    feat_major = pl.BlockSpec((D_MODEL, tm), lambda i, j: (0, i))
    small = lambda shape: pl.BlockSpec(shape, lambda i, j: (0, 0))
    tm_shape = jax.ShapeDtypeStruct((t, D_MODEL), BF16)
    fm_shape = jax.ShapeDtypeStruct((D_MODEL, t), BF16)
    return pl.pallas_call(
        _in_proj_kernel,
        grid=(t // tm, N_SEG),
        in_specs=[
            tok_major,
            small((1, D_MODEL)),
            pl.BlockSpec((D_MODEL, D_MODEL), lambda i, j: (0, j)),
            pl.BlockSpec((ROT_DIMS // 2, tm), lambda i, j: (0, i % n_pos_blocks)),
            pl.BlockSpec((ROT_DIMS // 2, tm), lambda i, j: (0, i % n_pos_blocks)),
            small((DIFF_DH, 1)), small((DIFF_DH, 1)), small((NA_DH, 1)), small((NA_DH, 1)),
        ],
        out_specs=[feat_major, tok_major, feat_major, feat_major, tok_major, feat_major,
                   tok_major, tok_major],
        out_shape=[fm_shape, tm_shape, fm_shape, fm_shape, tm_shape, fm_shape, tm_shape, tm_shape],
        scratch_shapes=[pltpu.VMEM((tm, D_MODEL), BF16)],
        compiler_params=_cparams(("parallel", "arbitrary")),
        name="in_proj",
    )(x2d, g1, w_in, cos_t, sin_t, gqd, gkd, gqn, gkn)


def _split_halves(q_t):
    row = lax.broadcasted_iota(jnp.int32, q_t.shape, 0)
    zero = jnp.zeros_like(q_t)
    return jnp.concatenate([jnp.where(row < DIFF_DH, q_t, zero),
                            jnp.where(row >= DIFF_DH, q_t, zero)], axis=1)


def _diff_attn_kernel(q_t_ref, k_ref, v_t_ref, km_ref, vm_t_ref, lq1_ref, lk1_ref, lq2_ref, lk2_ref,
                      subg_ref, o_ref, *, tk):
    tq = q_t_ref.shape[1]
    n = k_ref.shape[0]
    qz = _split_halves(q_t_ref[...])

    s = jnp.dot(km_ref[...], qz, preferred_element_type=F32)
    m = jnp.max(s, axis=0, keepdims=True)
    p = jnp.exp2(s - m)
    l = jnp.sum(p, axis=0, keepdims=True)
    acc = jnp.dot(vm_t_ref[...], p.astype(BF16), preferred_element_type=F32)

    for c in range(n // tk):
        s = jnp.dot(k_ref[c * tk:(c + 1) * tk, :], qz, preferred_element_type=F32)
        m_new = jnp.maximum(m, jnp.max(s, axis=0, keepdims=True))
        alpha = jnp.exp2(m - m_new)
        p = jnp.exp2(s - m_new)
        l = alpha * l + jnp.sum(p, axis=0, keepdims=True)
        acc = alpha * acc + jnp.dot(v_t_ref[:, c * tk:(c + 1) * tk], p.astype(BF16),
                                    preferred_element_type=F32)
        m = m_new

    lam = (jnp.exp(jnp.sum(lq1_ref[...] * lk1_ref[...], axis=-1, keepdims=True))
           - jnp.exp(jnp.sum(lq2_ref[...] * lk2_ref[...], axis=-1, keepdims=True)) + LAM_INIT)
    o = acc / l
    o_t = o[:, :tq] - lam * o[:, tq:]
    ms = jnp.mean(o_t * o_t, axis=0, keepdims=True)
    o_t = o_t * lax.rsqrt(ms + EPS) * subg_ref[...] * (1.0 - LAM_INIT)
    o_ref[...] = o_t.T.astype(BF16)


def _diff_attn(qd_t, kd, vd_t, kd_m, vd_t_m, lq1, lk1, lq2, lk2, subg, batch, n, tq, tk):
    nq = n // tq
    row = lambda shape: pl.BlockSpec(shape, lambda b, h, qi: (0, 0))
    return pl.pallas_call(
        functools.partial(_diff_attn_kernel, tk=tk),
        grid=(batch, DIFF_HEADS, nq),
        in_specs=[
            pl.BlockSpec((DIFF_VD, tq), lambda b, h, qi: (h, b * nq + qi)),
            pl.BlockSpec((n, DIFF_VD), lambda b, h, qi: (b, h)),
            pl.BlockSpec((DIFF_VD, n), lambda b, h, qi: (h, b)),
            pl.BlockSpec((N_META, DIFF_VD), lambda b, h, qi: (0, h)),
            pl.BlockSpec((DIFF_VD, N_META), lambda b, h, qi: (h, 0)),
            row((1, DIFF_DH)), row((1, DIFF_DH)), row((1, DIFF_DH)), row((1, DIFF_DH)),
            row((DIFF_VD, 1)),
        ],
        out_specs=pl.BlockSpec((tq, DIFF_VD), lambda b, h, qi: (b * nq + qi, h)),
        out_shape=jax.ShapeDtypeStruct((batch * n, D_MODEL), BF16),
        compiler_params=_cparams(("parallel", "parallel", "arbitrary")),
        name="diff_attn",
    )(qd_t, kd, vd_t, kd_m, vd_t_m, lq1, lk1, lq2, lk2, subg)


def _na_plan(rows):
    wh = min(NA_WIN_H, rows)
    assert wh == NA_WIN_H and rows % NA_ROWS_PER_STEP == 0 and rows >= NA_KEY_ROWS
    start = lambda r: int(np.clip(r - wh // 2, 0, rows - wh))
    sigs, plan = [], []
    for r0 in range(0, rows, NA_ROWS_PER_STEP):
        kr0 = int(np.clip(r0 - wh // 2, 0, rows - NA_KEY_ROWS))
        assert kr0 % 2 == 0
        sig = (r0 - kr0,) + tuple(start(r0 + g) - kr0 for g in range(NA_ROWS_PER_STEP))
        if sig not in sigs:
            sigs.append(sig)
        plan.append((kr0, sigs.index(sig)))
    return plan, sigs


def _na_bias_tables(rpb, meta_bias, sigs):
    nk = NA_KEY_ROWS * GRID_W
    nqr = NA_ROWS_PER_STEP * GRID_W
    jr = np.arange(nk) // GRID_W
    kc = np.arange(nk) % GRID_W
    g = np.arange(nqr) // GRID_W
    qc = np.arange(nqr) % GRID_W
    cs = np.clip(qc - NA_WIN_W // 2, 0, GRID_W - NA_WIN_W)
    col_ok = (kc[:, None] >= cs[None, :]) & (kc[:, None] < cs[None, :] + NA_WIN_W)
    dc = np.clip(kc[:, None] - qc[None, :], -(NA_WIN_W - 1), NA_WIN_W - 1) + NA_WIN_W - 1
    dr_all, ok_all = [], []
    for sig in sigs:
        dq, starts = sig[0], np.asarray(sig[1:])
        row_ok = (jr[:, None] >= starts[g][None, :]) & (jr[:, None] < starts[g][None, :] + NA_WIN_H)
        dr = jr[:, None] - dq - g[None, :] + NA_WIN_H - 1
        ok = row_ok & col_ok
        assert ((dr >= 0) & (dr < 2 * NA_WIN_H - 1))[ok].all()
        dr_all.append(np.clip(dr, 0, 2 * NA_WIN_H - 2))
        ok_all.append(ok)
    dr_all = np.stack(dr_all)
    ok_all = np.stack(ok_all)
    dc_all = np.broadcast_to(dc, dr_all.shape)
    bias = rpb.astype(F32)[:, dr_all, dc_all] * LOG2E
    bias = jnp.where(ok_all[None], bias, NEG)
    c = len(sigs)
    bias = bias.reshape(NA_HEADS // 2, 2, c, nk, nqr).transpose(0, 2, 3, 1, 4)
    bias = bias.reshape(NA_HEADS // 2, c, nk, 2 * nqr)
    mb = (meta_bias.astype(F32) * LOG2E).reshape(NA_HEADS // 2, 2, N_META)
    mb = jnp.broadcast_to(mb.transpose(0, 2, 1)[:, :, :, None], (NA_HEADS // 2, N_META, 2, nqr))
    return bias, mb.reshape(NA_HEADS // 2, N_META, 2 * nqr)


def _na_kernel(q_t_ref, k_ref, v_t_ref, km_ref, vm_t_ref, bias_ref, mbias_ref, o_ref, *, plan):
    nqr = NA_ROWS_PER_STEP * GRID_W
    nk = NA_KEY_ROWS * GRID_W
    km = km_ref[...]
    vm_t = vm_t_ref[...]
    mbias = mbias_ref[0]
    for rp, (kr0, cls) in enumerate(plan):
        q0 = rp * nqr
        k0 = kr0 * GRID_W
        qz = _split_halves(q_t_ref[:, q0:q0 + nqr])
        s = jnp.dot(k_ref[k0:k0 + nk, :], qz, preferred_element_type=F32) + bias_ref[0, cls]
        sm = jnp.dot(km, qz, preferred_element_type=F32) + mbias
        m = jnp.maximum(jnp.max(s, axis=0, keepdims=True), jnp.max(sm, axis=0, keepdims=True))
        p = jnp.exp2(s - m)
        pm = jnp.exp2(sm - m)
        l = jnp.sum(p, axis=0, keepdims=True) + jnp.sum(pm, axis=0, keepdims=True)
        acc = (jnp.dot(v_t_ref[:, k0:k0 + nk], p.astype(BF16), preferred_element_type=F32)
               + jnp.dot(vm_t, pm.astype(BF16), preferred_element_type=F32))
        o = acc / l
        o_t = jnp.concatenate([o[:NA_DH, :nqr], o[NA_DH:, nqr:]], axis=0)
        o_ref[q0:q0 + nqr, :] = o_t.T.astype(BF16)


def _na_attn(qn_t, kn, vn_t, kn_m, vn_t_m, bias, mbias, plan, batch, n):
    n_cls = bias.shape[1]
    hp = NA_HEADS // 2
    return pl.pallas_call(
        functools.partial(_na_kernel, plan=plan),
        grid=(batch, hp),
        in_specs=[
            pl.BlockSpec((2 * NA_DH, n), lambda b, h: (h, b)),
            pl.BlockSpec((n, 2 * NA_DH), lambda b, h: (b, h)),
            pl.BlockSpec((2 * NA_DH, n), lambda b, h: (h, b)),
            pl.BlockSpec((N_META, 2 * NA_DH), lambda b, h: (0, h)),
            pl.BlockSpec((2 * NA_DH, N_META), lambda b, h: (h, 0)),
            pl.BlockSpec((1, n_cls) + bias.shape[2:], lambda b, h: (h, 0, 0, 0)),
            pl.BlockSpec((1,) + mbias.shape[1:], lambda b, h: (h, 0, 0)),
        ],
        out_specs=pl.BlockSpec((n, 2 * NA_DH), lambda b, h: (b, h)),
        out_shape=jax.ShapeDtypeStruct((batch * n, D_MODEL), BF16),
        compiler_params=_cparams(("parallel", "arbitrary")),
        name="na_attn",
    )(qn_t, kn, vn_t, kn_m, vn_t_m, bias, mbias)


def _out_proj_kernel(oa_ref, ob_ref, ga_ref, gb_ref, x_ref, wd_ref, wn_ref, wo_ref, g2_ref, wr_ref,
                     br_ref, x1_ref, xn_ref, topi_ref, gate_ref):
    ya = jnp.dot(oa_ref[...], wd_ref[...], preferred_element_type=F32)
    yb = jnp.dot(ob_ref[...], wn_ref[...], preferred_element_type=F32)
    merged = ga_ref[...].astype(F32) * ya + gb_ref[...].astype(F32) * yb
    x1 = x_ref[...] + jnp.dot(merged.astype(BF16), wo_ref[...], preferred_element_type=F32)
    x1_ref[...] = x1
    ms = jnp.mean(x1 * x1, axis=-1, keepdims=True)
    xn = x1 * lax.rsqrt(ms + EPS) * g2_ref[...]
    xn_ref[...] = xn
    logits = jnp.dot(xn, wr_ref[...], preferred_element_type=F32,
                     precision=lax.Precision.HIGHEST) + br_ref[...]
    tm = logits.shape[0]
    lane_e = lax.broadcasted_iota(jnp.int32, logits.shape, 1)
    lane_o = lax.broadcasted_iota(jnp.int32, (tm, LANE), 1)
    topi = jnp.zeros((tm, LANE), jnp.int32)
    topv = jnp.zeros((tm, LANE), F32)
    v0 = None
    den = jnp.zeros((tm, 1), F32)
    for k in range(TOP_K):
        vk = jnp.max(logits, axis=-1, keepdims=True)
        ik = jnp.min(jnp.where(logits == vk, lane_e, N_EXPERTS), axis=-1, keepdims=True)
        logits = jnp.where(lane_e == ik, -jnp.inf, logits)
        if k == 0:
            v0 = vk
        ek = jnp.exp(vk - v0)
        den = den + ek
        topi = jnp.where(lane_o == k, ik, topi)
        topv = jnp.where(lane_o == k, ek, topv)
    topi_ref[...] = topi
    gate_ref[...] = topv / den


def _out_proj(o_a, o_b, ga, gb, x2d, wd, wn, wo, g2, wr, br, tm):
    t = x2d.shape[0]
    tok = pl.BlockSpec((tm, D_MODEL), lambda i: (i, 0))
    full = lambda shape: pl.BlockSpec(shape, lambda i: (0, 0))
    narrow = pl.BlockSpec((tm, LANE), lambda i: (i, 0))
    return pl.pallas_call(
        _out_proj_kernel,
        grid=(t // tm,),
        in_specs=[tok, tok, tok, tok, tok,
                  full((D_MODEL, D_MODEL)), full((D_MODEL, D_MODEL)), full((D_MODEL, D_MODEL)),
                  full((1, D_MODEL)), full((D_MODEL, N_EXPERTS)), full((1, N_EXPERTS))],
        out_specs=[tok, tok, narrow, narrow],
        out_shape=[jax.ShapeDtypeStruct((t, D_MODEL), F32), jax.ShapeDtypeStruct((t, D_MODEL), F32),
                   jax.ShapeDtypeStruct((t, LANE), jnp.int32), jax.ShapeDtypeStruct((t, LANE), F32)],
        compiler_params=_cparams(("parallel",)),
        name="out_proj",
    )(o_a, o_b, ga, gb, x2d, wd, wn, wo, g2, wr, br)


def _moe_kernel(meta_ref, be_ref, nv_ref, tok_ref, dst_ref, xn_hbm, g_ref, wgu_ref, bgu_ref, wd_ref,
                bd_ref, out_hbm, xbuf, ybuf, gsem, ssem, *, bm):
    i = pl.program_id(0)
    n_used = meta_ref[0]
    slot = i % 2

    def gather_copy(blk, s, r):
        tok = tok_ref[blk * bm + r]
        return pltpu.make_async_copy(xn_hbm.at[pl.ds(tok, 1)], xbuf.at[s, pl.ds(r, 1)], gsem.at[s])

    def scatter_copy(blk, s, r):
        dst = dst_ref[blk * bm + r]
        return pltpu.make_async_copy(ybuf.at[s, pl.ds(r, 1)], out_hbm.at[pl.ds(dst, 1)], ssem.at[s])

    def start_gather(blk, s):
        def body(r, c):
            gather_copy(blk, s, r).start()
            return c
        lax.fori_loop(0, bm, body, 0, unroll=8)

    def wait_gather(blk, s):
        def body(r, c):
            gather_copy(blk, s, r).wait()
            return c
        lax.fori_loop(0, bm, body, 0, unroll=8)

    def start_scatter(blk, s):
        def body(r, c):
            scatter_copy(blk, s, r).start()
            return c
        lax.fori_loop(0, nv_ref[blk], body, 0)

    def wait_scatter(blk, s):
        def body(r, c):
            scatter_copy(blk, s, r).wait()
            return c
        lax.fori_loop(0, nv_ref[blk], body, 0)

    @pl.when(i < n_used)
    def _():
        @pl.when(i == 0)
        def _():
            start_gather(0, 0)

        @pl.when(i + 1 < n_used)
        def _():
            start_gather(i + 1, 1 - slot)

        wait_gather(i, slot)
        x = xbuf[slot].astype(BF16)
        gu = jnp.dot(x, wgu_ref[0], preferred_element_type=F32) + bgu_ref[0]
        gate = jnp.minimum(gu[:, :D_FF], SWIGLU_LIMIT)
        up = jnp.clip(gu[:, D_FF:], -SWIGLU_LIMIT, SWIGLU_LIMIT)
        act = gate * jax.nn.sigmoid(SWIGLU_ALPHA * gate) * (up + 1.0)
        y = jnp.dot(act.astype(BF16), wd_ref[0], preferred_element_type=F32) + bd_ref[0]
        y = y * g_ref[...]

        @pl.when(i >= 2)
        def _():
            wait_scatter(i - 2, slot)

        ybuf[slot] = y
        start_scatter(i, slot)

        @pl.when(i == n_used - 1)
        def _():
            @pl.when(i >= 1)
            def _():
                wait_scatter(i - 1, 1 - slot)
            wait_scatter(i, slot)


def _moe(meta, block_e, block_nv, slot_tok, slot_dst, xn, slot_g, wgu, bgu, wd, bd, n_out_rows, bm):
    n_blocks = block_e.shape[0]
    grid_spec = pltpu.PrefetchScalarGridSpec(
        num_scalar_prefetch=5,
        grid=(n_blocks,),
        in_specs=[
            pl.BlockSpec(memory_space=pl.ANY),
            pl.BlockSpec((bm, 1), lambda i, mt, be, nv, tk, ds: (i, 0)),
            pl.BlockSpec((1, D_MODEL, 2 * D_FF), lambda i, mt, be, nv, tk, ds: (be[i], 0, 0)),
            pl.BlockSpec((1, 1, 2 * D_FF), lambda i, mt, be, nv, tk, ds: (be[i], 0, 0)),
            pl.BlockSpec((1, D_FF, D_MODEL), lambda i, mt, be, nv, tk, ds: (be[i], 0, 0)),
            pl.BlockSpec((1, 1, D_MODEL), lambda i, mt, be, nv, tk, ds: (be[i], 0, 0)),
        ],
        out_specs=pl.BlockSpec(memory_space=pl.ANY),
        scratch_shapes=[pltpu.VMEM((2, bm, D_MODEL), F32), pltpu.VMEM((2, bm, D_MODEL), F32),
                        pltpu.SemaphoreType.DMA((2,)), pltpu.SemaphoreType.DMA((2,))],
    )
    return pl.pallas_call(
        functools.partial(_moe_kernel, bm=bm),
        grid_spec=grid_spec,
        out_shape=jax.ShapeDtypeStruct((n_out_rows, D_MODEL), F32),
        compiler_params=_cparams(("arbitrary",)),
        name="moe_experts",
    )(meta, block_e, block_nv, slot_tok, slot_dst, xn, slot_g, wgu, bgu, wd, bd)


def _combine_kernel(x1_ref, y0_ref, y1_ref, y2_ref, y3_ref, o_ref):
    o_ref[...] = x1_ref[...] + ((y0_ref[0] + y1_ref[0]) + (y2_ref[0] + y3_ref[0]))


def _combine(x1, out4, tm):
    t = x1.shape[0]
    plane = lambda k: pl.BlockSpec((1, tm, D_MODEL), lambda i: (k, i, 0))
    return pl.pallas_call(
        _combine_kernel,
        grid=(t // tm,),
        in_specs=[pl.BlockSpec((tm, D_MODEL), lambda i: (i, 0))] + [plane(k) for k in range(TOP_K)],
        out_specs=pl.BlockSpec((tm, D_MODEL), lambda i: (i, 0)),
        out_shape=jax.ShapeDtypeStruct((t, D_MODEL), F32),
        compiler_params=_cparams(("parallel",)),
        name="moe_combine",
    )(x1, out4, out4, out4, out4)


def _route_plan(topi, gates, t, bm):
    a = t * TOP_K
    flat_e = topi.reshape(-1)
    flat_g = gates.reshape(-1)
    order = jnp.argsort(flat_e, stable=True).astype(jnp.int32)
    se = flat_e[order]
    counts = jnp.sum((flat_e[:, None] == jnp.arange(N_EXPERTS, dtype=jnp.int32)[None, :]).astype(jnp.int32),
                     axis=0)
    padded = (counts + bm - 1) // bm * bm
    pend = jnp.cumsum(padded)
    pstart = pend - padded
    ustart = jnp.cumsum(counts) - counts
    dest = pstart[se] + jnp.arange(a, dtype=jnp.int32) - ustart[se]
    n_blocks = -(-(a + N_EXPERTS * (bm - 1)) // bm)
    p = n_blocks * bm
    asg_row = (order % TOP_K) * t + order // TOP_K
    slot_dst = jnp.zeros((p,), jnp.int32).at[dest].set(asg_row)
    slot_tok = jnp.zeros((p,), jnp.int32).at[dest].set(order // TOP_K)
    slot_g = jnp.zeros((p,), F32).at[dest].set(flat_g[order])
    block_first = jnp.arange(n_blocks, dtype=jnp.int32) * bm
    block_e = jnp.minimum(jnp.searchsorted(pend, block_first, side='right'),
                          N_EXPERTS - 1).astype(jnp.int32)
    block_nv = jnp.clip(pstart[block_e] + counts[block_e] - block_first, 0, bm).astype(jnp.int32)
    meta = (pend[-1:] // bm).astype(jnp.int32)
    return meta, block_e, block_nv, slot_tok, slot_dst, slot_g.reshape(p, 1)


def _rope_tables_t(first_pos, count):
    inv = ROPE_THETA ** (-jnp.arange(0, ROT_DIMS, 2, dtype=F32) / ROT_DIMS)
    ang = (first_pos + jnp.arange(count, dtype=F32))[:, None] * inv[None, :]
    return jnp.cos(ang).T, jnp.sin(ang).T


def _pick(n, candidates):
    for c in candidates:
        if n % c == 0:
            return c
    raise ValueError(f"no tile for {n}")


def kernel(x_prompt, x_sample, meta_tokens, norm1_g, w_in, diff_q_g, diff_k_g, lam_q1, lam_k1, lam_q2, lam_k2, diff_sub_g, w_diff_out, na_q_g, na_k_g, na_rpb, na_meta_bias, w_na_out, w_o, norm2_g, w_router, b_router, w_gate_up, b_gate_up, w_down, b_down):
    l = 0
    col = lambda v: v.astype(F32).reshape(-1, 1)
    rowv = lambda v: v.astype(F32).reshape(1, -1)
    g1 = rowv(norm1_g[l])
    w_in_b = w_in[l].astype(BF16)
    gqd, gkd, gqn, gkn = col(diff_q_g[l]), col(diff_k_g[l]), col(na_q_g[l]), col(na_k_g[l])
    lq1, lk1, lq2, lk2 = rowv(lam_q1[l]), rowv(lam_k1[l]), rowv(lam_q2[l]), rowv(lam_k2[l])
    subg = col(diff_sub_g[l])
    wd_b, wn_b, wo_b = w_diff_out[l].astype(BF16), w_na_out[l].astype(BF16), w_o[l].astype(BF16)
    g2 = rowv(norm2_g[l])
    wr = w_router[l].astype(F32)
    br = rowv(b_router[l])
    wgu_b = w_gate_up[l].astype(BF16)
    bgu = b_gate_up[l].astype(F32).reshape(N_EXPERTS, 1, 2 * D_FF)
    wdn_b = w_down[l].astype(BF16)
    bdn = b_down[l].astype(F32).reshape(N_EXPERTS, 1, D_MODEL)

    meta_pad = jnp.zeros((LANE, D_MODEL), F32).at[:N_META].set(meta_tokens.astype(F32))
    cos_m, sin_m = _rope_tables_t(0.0, LANE)
    _, kd_m, vd_t_m, _, kn_m, vn_t_m, _, _ = _in_proj(meta_pad, g1, w_in_b, cos_m, sin_m,
                                                      gqd, gkd, gqn, gkn, LANE)
    kd_m, kn_m = kd_m[:N_META], kn_m[:N_META]
    vd_t_m, vn_t_m = vd_t_m[:, :N_META], vn_t_m[:, :N_META]

    def trunk(x):
        batch, n, _ = x.shape
        t = batch * n
        x2d = x.reshape(t, D_MODEL)
        tm = _pick(n, (512, 256, 128))
        cos_t, sin_t = _rope_tables_t(float(N_META), n)
        qd_t, kd, vd_t, qn_t, kn, vn_t, ga, gb = _in_proj(x2d, g1, w_in_b, cos_t, sin_t,
                                                         gqd, gkd, gqn, gkn, tm)
        tq = _pick(n, (512, 256, 128))
        o_a = _diff_attn(qd_t, kd, vd_t, kd_m, vd_t_m, lq1, lk1, lq2, lk2, subg, batch, n, tq, tq)
        plan, sigs = _na_plan(n // GRID_W)
        bias, mbias = _na_bias_tables(na_rpb[l], na_meta_bias[l], sigs)
        o_b = _na_attn(qn_t, kn, vn_t, kn_m, vn_t_m, bias, mbias, plan, batch, n)
        x1, xn, topi, gates = _out_proj(o_a, o_b, ga, gb, x2d, wd_b, wn_b, wo_b, g2, wr, br, tm)
        bm = 256
        meta, block_e, block_nv, slot_tok, slot_dst, slot_g = _route_plan(
            topi[:, :TOP_K], gates[:, :TOP_K], t, bm)
        out4 = _moe(meta, block_e, block_nv, slot_tok, slot_dst, xn, slot_g, wgu_b, bgu, wdn_b, bdn,
                    TOP_K * t, bm)
        y = _combine(x1, out4.reshape(TOP_K, t, D_MODEL), tm)
        return y.reshape(batch, n, D_MODEL)

    return (trunk(x_prompt), trunk(x_sample))
```

```python
import functools
import math

import numpy as np
import jax
import jax.numpy as jnp
from jax import lax
from jax.experimental import pallas as pl
from jax.experimental.pallas import tpu as pltpu

D_MODEL = 1024
N_META = 16
GRID_W = 64
DIFF_HEADS = 8
DIFF_DH = 64
DIFF_VD = 2 * DIFF_DH
NA_HEADS = 16
NA_DH = 64
NA_WIN_H = 8
NA_WIN_W = 16
ROT_DIMS = DIFF_DH // 4
ROPE_THETA = 500000.0
N_EXPERTS = 32
TOP_K = 4
D_FF = D_MODEL
SWIGLU_ALPHA = 1.702
SWIGLU_LIMIT = 7.0
EPS = 1e-6
NEG = -1e30
LAM_INIT = 0.8 - 0.6 * math.exp(-0.3 * 0)
LOG2E = 1.4426950408889634
Q_SCALE = (DIFF_DH ** -0.5) * LOG2E

N_SEG = 8
LANE = 128
NA_ROWS_PER_STEP = 2
NA_KEY_ROWS = NA_WIN_H + NA_ROWS_PER_STEP
VMEM_LIMIT = 56 * 1024 * 1024

F32 = jnp.float32
BF16 = jnp.bfloat16


def _cparams(sem):
    return pltpu.CompilerParams(dimension_semantics=sem, vmem_limit_bytes=VMEM_LIMIT)


def _head_norm_t(y_t, g_ref, cos, sin, scale):
    tm = y_t.shape[1]
    y3 = y_t.reshape(D_MODEL // DIFF_DH, DIFF_DH, tm)
    ms = jnp.mean(y3 * y3, axis=1, keepdims=True)
    y3 = y3 * lax.rsqrt(ms + EPS) * g_ref[...].reshape(1, DIFF_DH, 1)
    if cos is not None:
        half = ROT_DIMS // 2
        x1 = y3[:, 0:half, :]
        x2 = y3[:, half:ROT_DIMS, :]
        c = cos[None]
        s = sin[None]
        y3 = jnp.concatenate([x1 * c - x2 * s, x2 * c + x1 * s, y3[:, ROT_DIMS:, :]], axis=1)
    if scale != 1.0:
        y3 = y3 * scale
    return y3.reshape(D_MODEL, tm)


def _in_proj_kernel(x_ref, g1_ref, w_ref, cos_ref, sin_ref, gqd_ref, gkd_ref, gqn_ref, gkn_ref,
                    qd_t_ref, kd_ref, vd_t_ref, qn_t_ref, kn_ref, vn_t_ref, ga_ref, gb_ref, h_ref):
    j = pl.program_id(1)

    @pl.when(j == 0)
    def _():
        x = x_ref[...]
        ms = jnp.mean(x * x, axis=-1, keepdims=True)
        h_ref[...] = (x * lax.rsqrt(ms + EPS) * g1_ref[...]).astype(BF16)

    y = jnp.dot(h_ref[...], w_ref[...], preferred_element_type=F32)

    @pl.when(j == 0)
    def _():
        qd_t_ref[...] = _head_norm_t(y.T, gqd_ref, cos_ref[...], sin_ref[...], Q_SCALE).astype(BF16)

    @pl.when(j == 1)
    def _():
        kd_ref[...] = _head_norm_t(y.T, gkd_ref, cos_ref[...], sin_ref[...], 1.0).T.astype(BF16)

    @pl.when(j == 2)
    def _():
        vd_t_ref[...] = y.T.astype(BF16)

    @pl.when(j == 3)
    def _():
        qn_t_ref[...] = _head_norm_t(y.T, gqn_ref, None, None, Q_SCALE).astype(BF16)

    @pl.when(j == 4)
    def _():
        kn_ref[...] = _head_norm_t(y.T, gkn_ref, None, None, 1.0).T.astype(BF16)

    @pl.when(j == 5)
    def _():
        vn_t_ref[...] = y.T.astype(BF16)

    @pl.when(j == 6)
    def _():
        ga_ref[...] = jax.nn.sigmoid(y).astype(BF16)

    @pl.when(j == 7)
    def _():
        gb_ref[...] = jax.nn.sigmoid(y).astype(BF16)


def _in_proj(x2d, g1, w_in, cos_t, sin_t, gqd, gkd, gqn, gkn, tm):
    t = x2d.shape[0]
    n_pos_blocks = cos_t.shape[1] // tm
    tok_major = pl.BlockSpec((tm, D_MODEL), lambda i, j: (i, 0))
    feat_major = pl.BlockSpec((D_MODEL, tm), lambda i, j: (0, i))
    small = lambda shape: pl.BlockSpec(shape, lambda i, j: (0, 0))
    tm_shape = jax.ShapeDtypeStruct((t, D_MODEL), BF16)
    fm_shape = jax.ShapeDtypeStruct((D_MODEL, t), BF16)
    return pl.pallas_call(
        _in_proj_kernel,
        grid=(t // tm, N_SEG),
        in_specs=[
            tok_major,
            small((1, D_MODEL)),
            pl.BlockSpec((D_MODEL, D_MODEL), lambda i, j: (0, j)),
            pl.BlockSpec((ROT_DIMS // 2, tm), lambda i, j: (0, i % n_pos_blocks)),
            pl.BlockSpec((ROT_DIMS // 2, tm), lambda i, j: (0, i % n_pos_blocks)),
            small((DIFF_DH, 1)), small((DIFF_DH, 1)), small((NA_DH, 1)), small((NA_DH, 1)),
        ],
        out_specs=[feat_major, tok_major, feat_major, feat_major, tok_major, feat_major,
                   tok_major, tok_major],
        out_shape=[fm_shape, tm_shape, fm_shape, fm_shape, tm_shape, fm_shape, tm_shape, tm_shape],
        scratch_shapes=[pltpu.VMEM((tm, D_MODEL), BF16)],
        compiler_params=_cparams(("parallel", "arbitrary")),
        name="in_proj",
    )(x2d, g1, w_in, cos_t, sin_t, gqd, gkd, gqn, gkn)


def _split_halves(q_t):
    row = lax.broadcasted_iota(jnp.int32, q_t.shape, 0)
    zero = jnp.zeros_like(q_t)
    return jnp.concatenate([jnp.where(row < DIFF_DH, q_t, zero),
                            jnp.where(row >= DIFF_DH, q_t, zero)], axis=1)


def _diff_attn_kernel(q_t_ref, k_ref, v_t_ref, km_ref, vm_t_ref, lq1_ref, lk1_ref, lq2_ref, lk2_ref,
                      subg_ref, o_ref, *, tk):
    tq = q_t_ref.shape[1]
    n = k_ref.shape[0]
    qz = _split_halves(q_t_ref[...])

    s = jnp.dot(km_ref[...], qz, preferred_element_type=F32)
    m = jnp.max(s, axis=0, keepdims=True)
    p = jnp.exp2(s - m)
    l = jnp.sum(p, axis=0, keepdims=True)
    acc = jnp.dot(vm_t_ref[...], p.astype(BF16), preferred_element_type=F32)

    for c in range(n // tk):
        s = jnp.dot(k_ref[c * tk:(c + 1) * tk, :], qz, preferred_element_type=F32)
        m_new = jnp.maximum(m, jnp.max(s, axis=0, keepdims=True))
        alpha = jnp.exp2(m - m_new)
        p = jnp.exp2(s - m_new)
        l = alpha * l + jnp.sum(p, axis=0, keepdims=True)
        acc = alpha * acc + jnp.dot(v_t_ref[:, c * tk:(c + 1) * tk], p.astype(BF16),
                                    preferred_element_type=F32)
        m = m_new

    lam = (jnp.exp(jnp.sum(lq1_ref[...] * lk1_ref[...], axis=-1, keepdims=True))
           - jnp.exp(jnp.sum(lq2_ref[...] * lk2_ref[...], axis=-1, keepdims=True)) + LAM_INIT)
    o = acc / l
    o_t = o[:, :tq] - lam * o[:, tq:]
    ms = jnp.mean(o_t * o_t, axis=0, keepdims=True)
    o_t = o_t * lax.rsqrt(ms + EPS) * subg_ref[...] * (1.0 - LAM_INIT)
    o_ref[...] = o_t.T.astype(BF16)


def _diff_attn(qd_t, kd, vd_t, kd_m, vd_t_m, lq1, lk1, lq2, lk2, subg, batch, n, tq, tk):
    nq = n // tq
    row = lambda shape: pl.BlockSpec(shape, lambda b, h, qi: (0, 0))
    return pl.pallas_call(
        functools.partial(_diff_attn_kernel, tk=tk),
        grid=(batch, DIFF_HEADS, nq),
        in_specs=[
            pl.BlockSpec((DIFF_VD, tq), lambda b, h, qi: (h, b * nq + qi)),
            pl.BlockSpec((n, DIFF_VD), lambda b, h, qi: (b, h)),
            pl.BlockSpec((DIFF_VD, n), lambda b, h, qi: (h, b)),
            pl.BlockSpec((N_META, DIFF_VD), lambda b, h, qi: (0, h)),
            pl.BlockSpec((DIFF_VD, N_META), lambda b, h, qi: (h, 0)),
            row((1, DIFF_DH)), row((1, DIFF_DH)), row((1, DIFF_DH)), row((1, DIFF_DH)),
            row((DIFF_VD, 1)),
        ],
        out_specs=pl.BlockSpec((tq, DIFF_VD), lambda b, h, qi: (b * nq + qi, h)),
        out_shape=jax.ShapeDtypeStruct((batch * n, D_MODEL), BF16),
        compiler_params=_cparams(("parallel", "parallel", "arbitrary")),
        name="diff_attn",
    )(qd_t, kd, vd_t, kd_m, vd_t_m, lq1, lk1, lq2, lk2, subg)


def _na_plan(rows):
    wh = min(NA_WIN_H, rows)
    assert wh == NA_WIN_H and rows % NA_ROWS_PER_STEP == 0 and rows >= NA_KEY_ROWS
    start = lambda r: int(np.clip(r - wh // 2, 0, rows - wh))
    sigs, plan = [], []
    for r0 in range(0, rows, NA_ROWS_PER_STEP):
        kr0 = int(np.clip(r0 - wh // 2, 0, rows - NA_KEY_ROWS))
        assert kr0 % 2 == 0
        sig = (r0 - kr0,) + tuple(start(r0 + g) - kr0 for g in range(NA_ROWS_PER_STEP))
        if sig not in sigs:
            sigs.append(sig)
        plan.append((kr0, sigs.index(sig)))
    return plan, sigs


def _na_bias_tables(rpb, meta_bias, sigs):
    nk = NA_KEY_ROWS * GRID_W
    nqr = NA_ROWS_PER_STEP * GRID_W
    kc = np.arange(GRID_W)[:, None]
    qc = np.arange(GRID_W)[None, :]
    cs = np.clip(qc - NA_WIN_W // 2, 0, GRID_W - NA_WIN_W)
    col_ok = (kc >= cs) & (kc < cs + NA_WIN_W)
    dc = np.clip(kc - qc, -(NA_WIN_W - 1), NA_WIN_W - 1) + NA_WIN_W - 1
    toe = jnp.where(col_ok, rpb.astype(F32)[:, :, dc] * LOG2E, NEG)
    neg = jnp.full((NA_HEADS, GRID_W, GRID_W), NEG, F32)
    per_class = []
    for sig in sigs:
        dq, starts = sig[0], sig[1:]
        key_rows = []
        for jr in range(NA_KEY_ROWS):
            blocks = []
            for g in range(NA_ROWS_PER_STEP):
                in_window = starts[g] <= jr < starts[g] + NA_WIN_H
                dr = jr - dq - g + NA_WIN_H - 1
                assert not in_window or 0 <= dr < 2 * NA_WIN_H - 1
                blocks.append(toe[:, dr] if in_window else neg)
            key_rows.append(jnp.concatenate(blocks, axis=-1))
        per_class.append(jnp.concatenate(key_rows, axis=-2))
    bias = jnp.stack(per_class, axis=1)
    c = len(sigs)
    bias = bias.reshape(NA_HEADS // 2, 2, c, nk, nqr).transpose(0, 2, 3, 1, 4)
    bias = bias.reshape(NA_HEADS // 2, c, nk, 2 * nqr)
    mb = (meta_bias.astype(F32) * LOG2E).reshape(NA_HEADS // 2, 2, N_META)
    mb = jnp.broadcast_to(mb.transpose(0, 2, 1)[:, :, :, None], (NA_HEADS // 2, N_META, 2, nqr))
    return bias, mb.reshape(NA_HEADS // 2, N_META, 2 * nqr)


def _na_kernel(q_t_ref, k_ref, v_t_ref, km_ref, vm_t_ref, bias_ref, mbias_ref, o_ref, *, plan):
    nqr = NA_ROWS_PER_STEP * GRID_W
    nk = NA_KEY_ROWS * GRID_W
    km = km_ref[...]
    vm_t = vm_t_ref[...]
    mbias = mbias_ref[0]
    for rp, (kr0, cls) in enumerate(plan):
        q0 = rp * nqr
        k0 = kr0 * GRID_W
        qz = _split_halves(q_t_ref[:, q0:q0 + nqr])
        s = jnp.dot(k_ref[k0:k0 + nk, :], qz, preferred_element_type=F32) + bias_ref[0, cls]
        sm = jnp.dot(km, qz, preferred_element_type=F32) + mbias
        m = jnp.maximum(jnp.max(s, axis=0, keepdims=True), jnp.max(sm, axis=0, keepdims=True))
        p = jnp.exp2(s - m)
        pm = jnp.exp2(sm - m)
        l = jnp.sum(p, axis=0, keepdims=True) + jnp.sum(pm, axis=0, keepdims=True)
        acc = (jnp.dot(v_t_ref[:, k0:k0 + nk], p.astype(BF16), preferred_element_type=F32)
               + jnp.dot(vm_t, pm.astype(BF16), preferred_element_type=F32))
        o = acc / l
        o_t = jnp.concatenate([o[:NA_DH, :nqr], o[NA_DH:, nqr:]], axis=0)
        o_ref[q0:q0 + nqr, :] = o_t.T.astype(BF16)


def _na_attn(qn_t, kn, vn_t, kn_m, vn_t_m, bias, mbias, plan, batch, n):
    n_cls = bias.shape[1]
    hp = NA_HEADS // 2
    return pl.pallas_call(
        functools.partial(_na_kernel, plan=plan),
        grid=(batch, hp),
        in_specs=[
            pl.BlockSpec((2 * NA_DH, n), lambda b, h: (h, b)),
            pl.BlockSpec((n, 2 * NA_DH), lambda b, h: (b, h)),
            pl.BlockSpec((2 * NA_DH, n), lambda b, h: (h, b)),
            pl.BlockSpec((N_META, 2 * NA_DH), lambda b, h: (0, h)),
            pl.BlockSpec((2 * NA_DH, N_META), lambda b, h: (h, 0)),
            pl.BlockSpec((1, n_cls) + bias.shape[2:], lambda b, h: (h, 0, 0, 0)),
            pl.BlockSpec((1,) + mbias.shape[1:], lambda b, h: (h, 0, 0)),
        ],
        out_specs=pl.BlockSpec((n, 2 * NA_DH), lambda b, h: (b, h)),
        out_shape=jax.ShapeDtypeStruct((batch * n, D_MODEL), BF16),
        compiler_params=_cparams(("parallel", "arbitrary")),
        name="na_attn",
    )(qn_t, kn, vn_t, kn_m, vn_t_m, bias, mbias)


def _out_proj_kernel(oa_ref, ob_ref, ga_ref, gb_ref, x_ref, wd_ref, wn_ref, wo_ref, g2_ref, wr_ref,
                     br_ref, x1_ref, xn_ref, topi_ref, gate_ref):
    ya = jnp.dot(oa_ref[...], wd_ref[...], preferred_element_type=F32)
    yb = jnp.dot(ob_ref[...], wn_ref[...], preferred_element_type=F32)
    merged = ga_ref[...].astype(F32) * ya + gb_ref[...].astype(F32) * yb
    x1 = x_ref[...] + jnp.dot(merged.astype(BF16), wo_ref[...], preferred_element_type=F32)
    x1_ref[...] = x1
    ms = jnp.mean(x1 * x1, axis=-1, keepdims=True)
    xn = x1 * lax.rsqrt(ms + EPS) * g2_ref[...]
    xn_ref[...] = xn
    logits = jnp.dot(xn, wr_ref[...], preferred_element_type=F32,
                     precision=lax.Precision.HIGHEST) + br_ref[...]
    tm = logits.shape[0]
    lane_e = lax.broadcasted_iota(jnp.int32, logits.shape, 1)
    lane_o = lax.broadcasted_iota(jnp.int32, (tm, LANE), 1)
    topi = jnp.zeros((tm, LANE), jnp.int32)
    topv = jnp.zeros((tm, LANE), F32)
    v0 = None
    den = jnp.zeros((tm, 1), F32)
    for k in range(TOP_K):
        vk = jnp.max(logits, axis=-1, keepdims=True)
        ik = jnp.min(jnp.where(logits == vk, lane_e, N_EXPERTS), axis=-1, keepdims=True)
        logits = jnp.where(lane_e == ik, -jnp.inf, logits)
        if k == 0:
            v0 = vk
        ek = jnp.exp(vk - v0)
        den = den + ek
        topi = jnp.where(lane_o == k, ik, topi)
        topv = jnp.where(lane_o == k, ek, topv)
    topi_ref[...] = topi
    gate_ref[...] = topv / den


def _out_proj(o_a, o_b, ga, gb, x2d, wd, wn, wo, g2, wr, br, tm):
    t = x2d.shape[0]
    tok = pl.BlockSpec((tm, D_MODEL), lambda i: (i, 0))
    full = lambda shape: pl.BlockSpec(shape, lambda i: (0, 0))
    narrow = pl.BlockSpec((tm, LANE), lambda i: (i, 0))
    return pl.pallas_call(
        _out_proj_kernel,
        grid=(t // tm,),
        in_specs=[tok, tok, tok, tok, tok,
                  full((D_MODEL, D_MODEL)), full((D_MODEL, D_MODEL)), full((D_MODEL, D_MODEL)),
                  full((1, D_MODEL)), full((D_MODEL, N_EXPERTS)), full((1, N_EXPERTS))],
        out_specs=[tok, tok, narrow, narrow],
        out_shape=[jax.ShapeDtypeStruct((t, D_MODEL), F32), jax.ShapeDtypeStruct((t, D_MODEL), F32),
                   jax.ShapeDtypeStruct((t, LANE), jnp.int32), jax.ShapeDtypeStruct((t, LANE), F32)],
        compiler_params=_cparams(("parallel",)),
        name="out_proj",
    )(o_a, o_b, ga, gb, x2d, wd, wn, wo, g2, wr, br)


def _route_plan(topi, t, bm):
    e_ids = jnp.arange(N_EXPERTS, dtype=jnp.int32)
    onehot = topi[:, :, None] == e_ids[None, None, :]
    routed = jnp.sum(onehot, axis=1).astype(jnp.int32)
    incl = jnp.cumsum(routed, axis=0)
    counts = incl[-1]
    padded = (counts + bm - 1) // bm * bm
    pend = jnp.cumsum(padded)
    pstart = pend - padded
    slot_of = (pstart[None, :] + incl - routed)[:, None, :]
    pos = jnp.sum(jnp.where(onehot, slot_of, 0), axis=-1).astype(jnp.int32)
    n_blocks = -(-(t * TOP_K + N_EXPERTS * (bm - 1)) // bm)
    block_first = jnp.arange(n_blocks, dtype=jnp.int32) * bm
    block_e = jnp.minimum(jnp.searchsorted(pend, block_first, side='right'),
                          N_EXPERTS - 1).astype(jnp.int32)
    n_used = (pend[-1:] // bm).astype(jnp.int32)
    pad_block_row = jnp.where(padded > 0, pend - bm, -1).astype(jnp.int32)
    return pos.reshape(-1), block_e, n_used, pad_block_row


def _dispatch_kernel(pos_ref, padrow_ref, nused_ref, xn_hbm, xs_hbm, zbuf, zsem, sem, *, tm, bm, n_blocks):
    i = pl.program_id(0)
    n_steps = pl.num_programs(0)
    slot = i % 2

    def zero_copy(row0):
        return pltpu.make_async_copy(zbuf, xs_hbm.at[pl.ds(pl.multiple_of(row0, bm), bm)], zsem)

    def for_each_zero_block(fn):
        for e in range(N_EXPERTS):
            @pl.when(padrow_ref[e] >= 0)
            def _():
                fn(zero_copy(padrow_ref[e]))

        def tail(b, c):
            fn(zero_copy(b * bm))
            return c
        lax.fori_loop(nused_ref[0], n_blocks, tail, 0)

    @pl.when(i == 0)
    def _():
        zbuf[...] = jnp.zeros_like(zbuf)
        for_each_zero_block(lambda cp: cp.start())
        for_each_zero_block(lambda cp: cp.wait())

    def row_copy(step, s, r, k):
        tok = step * tm + r
        return pltpu.make_async_copy(xn_hbm.at[pl.ds(tok, 1)],
                                     xs_hbm.at[pl.ds(pos_ref[tok * TOP_K + k], 1)], sem.at[s])

    def for_each_row(step, s, fn):
        def body(r, c):
            for k in range(TOP_K):
                fn(row_copy(step, s, r, k))
            return c
        lax.fori_loop(0, tm, body, 0, unroll=4)

    for_each_row(i, slot, lambda cp: cp.start())

    @pl.when(i >= 1)
    def _():
        for_each_row(i - 1, 1 - slot, lambda cp: cp.wait())

    @pl.when(i == n_steps - 1)
    def _():
        for_each_row(i, slot, lambda cp: cp.wait())


def _dispatch(pos, pad_block_row, n_used, xn, n_blocks, tm, bm):
    t = xn.shape[0]
    grid_spec = pltpu.PrefetchScalarGridSpec(
        num_scalar_prefetch=3,
        grid=(t // tm,),
        in_specs=[pl.BlockSpec(memory_space=pl.ANY)],
        out_specs=pl.BlockSpec(memory_space=pl.ANY),
        scratch_shapes=[pltpu.VMEM((bm, D_MODEL), F32), pltpu.SemaphoreType.DMA(()),
                        pltpu.SemaphoreType.DMA((2,))],
    )
    return pl.pallas_call(
        functools.partial(_dispatch_kernel, tm=tm, bm=bm, n_blocks=n_blocks),
        grid_spec=grid_spec,
        out_shape=jax.ShapeDtypeStruct((n_blocks * bm, D_MODEL), F32),
        compiler_params=_cparams(("arbitrary",)),
        name="moe_dispatch",
    )(pos, pad_block_row, n_used, xn)


def _moe_kernel(nused_ref, be_ref, xs_ref, wgu_ref, bgu_ref, wd_ref, bd_ref, ys_ref):
    i = pl.program_id(0)

    @pl.when(i < nused_ref[0])
    def _():
        x = xs_ref[...].astype(BF16)
        gu = jnp.dot(x, wgu_ref[0], preferred_element_type=F32) + bgu_ref[0]
        gate = jnp.minimum(gu[:, :D_FF], SWIGLU_LIMIT)
        up = jnp.clip(gu[:, D_FF:], -SWIGLU_LIMIT, SWIGLU_LIMIT)
        act = gate * jax.nn.sigmoid(SWIGLU_ALPHA * gate) * (up + 1.0)
        ys_ref[...] = jnp.dot(act.astype(BF16), wd_ref[0], preferred_element_type=F32) + bd_ref[0]

    @pl.when(i >= nused_ref[0])
    def _():
        ys_ref[...] = jnp.zeros_like(ys_ref)


def _moe(n_used, block_e, xs, wgu, bgu, wd, bd, bm):
    n_blocks = block_e.shape[0]
    grid_spec = pltpu.PrefetchScalarGridSpec(
        num_scalar_prefetch=2,
        grid=(n_blocks,),
        in_specs=[
            pl.BlockSpec((bm, D_MODEL), lambda i, nu, be: (i, 0)),
            pl.BlockSpec((1, D_MODEL, 2 * D_FF), lambda i, nu, be: (be[i], 0, 0)),
            pl.BlockSpec((1, 1, 2 * D_FF), lambda i, nu, be: (be[i], 0, 0)),
            pl.BlockSpec((1, D_FF, D_MODEL), lambda i, nu, be: (be[i], 0, 0)),
            pl.BlockSpec((1, 1, D_MODEL), lambda i, nu, be: (be[i], 0, 0)),
        ],
        out_specs=pl.BlockSpec((bm, D_MODEL), lambda i, nu, be: (i, 0)),
    )
    return pl.pallas_call(
        _moe_kernel,
        grid_spec=grid_spec,
        out_shape=jax.ShapeDtypeStruct((n_blocks * bm, D_MODEL), F32),
        compiler_params=_cparams(("arbitrary",)),
        name="moe_experts",
    )(n_used, block_e, xs, wgu, bgu, wd, bd)


def _combine_kernel(pos_ref, x1_ref, gate_ref, ys_hbm, o_ref, buf, sem, *, tm):
    i = pl.program_id(0)
    n_steps = pl.num_programs(0)
    slot = i % 2

    def row_copy(step, s, r, k):
        tok = step * tm + r
        return pltpu.make_async_copy(ys_hbm.at[pl.ds(pos_ref[tok * TOP_K + k], 1)],
                                     buf.at[s, k, pl.ds(r, 1)], sem.at[s])

    def for_each_row(step, s, fn):
        def body(r, c):
            for k in range(TOP_K):
                fn(row_copy(step, s, r, k))
            return c
        lax.fori_loop(0, tm, body, 0, unroll=4)

    @pl.when(i == 0)
    def _():
        for_each_row(0, 0, lambda cp: cp.start())

    @pl.when(i + 1 < n_steps)
    def _():
        for_each_row(i + 1, 1 - slot, lambda cp: cp.start())

    for_each_row(i, slot, lambda cp: cp.wait())
    g = gate_ref[...]
    acc = x1_ref[...]
    for k in range(TOP_K):
        acc = acc + g[:, k:k + 1] * buf[slot, k]
    o_ref[...] = acc


def _combine(pos, x1, gates, ys, tm):
    t = x1.shape[0]
    grid_spec = pltpu.PrefetchScalarGridSpec(
        num_scalar_prefetch=1,
        grid=(t // tm,),
        in_specs=[pl.BlockSpec((tm, D_MODEL), lambda i, ps: (i, 0)),
                  pl.BlockSpec((tm, LANE), lambda i, ps: (i, 0)),
                  pl.BlockSpec(memory_space=pl.ANY)],
        out_specs=pl.BlockSpec((tm, D_MODEL), lambda i, ps: (i, 0)),
        scratch_shapes=[pltpu.VMEM((2, TOP_K, tm, D_MODEL), F32), pltpu.SemaphoreType.DMA((2,))],
    )
    return pl.pallas_call(
        functools.partial(_combine_kernel, tm=tm),
        grid_spec=grid_spec,
        out_shape=jax.ShapeDtypeStruct((t, D_MODEL), F32),
        compiler_params=_cparams(("arbitrary",)),
        name="moe_combine",
    )(pos, x1, gates, ys)


def _rope_tables_t(first_pos, count):
    inv = ROPE_THETA ** (-jnp.arange(0, ROT_DIMS, 2, dtype=F32) / ROT_DIMS)
    ang = (first_pos + jnp.arange(count, dtype=F32))[:, None] * inv[None, :]
    return jnp.cos(ang).T, jnp.sin(ang).T


def _pick(n, candidates):
    for c in candidates:
        if n % c == 0:
            return c
    raise ValueError(f"no tile for {n}")


def kernel(x_prompt, x_sample, meta_tokens, norm1_g, w_in, diff_q_g, diff_k_g, lam_q1, lam_k1, lam_q2, lam_k2, diff_sub_g, w_diff_out, na_q_g, na_k_g, na_rpb, na_meta_bias, w_na_out, w_o, norm2_g, w_router, b_router, w_gate_up, b_gate_up, w_down, b_down):
    l = 0
    col = lambda v: v.astype(F32).reshape(-1, 1)
    rowv = lambda v: v.astype(F32).reshape(1, -1)
    g1 = rowv(norm1_g[l])
    w_in_b = w_in[l].astype(BF16)
    gqd, gkd, gqn, gkn = col(diff_q_g[l]), col(diff_k_g[l]), col(na_q_g[l]), col(na_k_g[l])
    lq1, lk1, lq2, lk2 = rowv(lam_q1[l]), rowv(lam_k1[l]), rowv(lam_q2[l]), rowv(lam_k2[l])
    subg = col(diff_sub_g[l])
    wd_b, wn_b, wo_b = w_diff_out[l].astype(BF16), w_na_out[l].astype(BF16), w_o[l].astype(BF16)
    g2 = rowv(norm2_g[l])
    wr = w_router[l].astype(F32)
    br = rowv(b_router[l])
    wgu_b = w_gate_up[l].astype(BF16)
    bgu = b_gate_up[l].astype(F32).reshape(N_EXPERTS, 1, 2 * D_FF)
    wdn_b = w_down[l].astype(BF16)
    bdn = b_down[l].astype(F32).reshape(N_EXPERTS, 1, D_MODEL)

    meta_pad = jnp.zeros((LANE, D_MODEL), F32).at[:N_META].set(meta_tokens.astype(F32))
    cos_m, sin_m = _rope_tables_t(0.0, LANE)
    _, kd_m, vd_t_m, _, kn_m, vn_t_m, _, _ = _in_proj(meta_pad, g1, w_in_b, cos_m, sin_m,
                                                      gqd, gkd, gqn, gkn, LANE)
    kd_m, kn_m = kd_m[:N_META], kn_m[:N_META]
    vd_t_m, vn_t_m = vd_t_m[:, :N_META], vn_t_m[:, :N_META]

    def trunk(x):
        batch, n, _ = x.shape
        t = batch * n
        x2d = x.reshape(t, D_MODEL)
        tm = _pick(n, (512, 256, 128))
        cos_t, sin_t = _rope_tables_t(float(N_META), n)
        qd_t, kd, vd_t, qn_t, kn, vn_t, ga, gb = _in_proj(x2d, g1, w_in_b, cos_t, sin_t,
                                                         gqd, gkd, gqn, gkn, tm)
        tq = _pick(n, (512, 256, 128))
        o_a = _diff_attn(qd_t, kd, vd_t, kd_m, vd_t_m, lq1, lk1, lq2, lk2, subg, batch, n, tq, tq)
        plan, sigs = _na_plan(n // GRID_W)
        if tuple(sigs) not in bias_cache:
            bias_cache[tuple(sigs)] = _na_bias_tables(na_rpb[l], na_meta_bias[l], sigs)
        bias, mbias = bias_cache[tuple(sigs)]
        o_b = _na_attn(qn_t, kn, vn_t, kn_m, vn_t_m, bias, mbias, plan, batch, n)
        x1, xn, topi, gates = _out_proj(o_a, o_b, ga, gb, x2d, wd_b, wn_b, wo_b, g2, wr, br, tm)
        bm = 256
        pos, block_e, n_used, pad_block_row = _route_plan(topi[:, :TOP_K], t, bm)
        xs = _dispatch(pos, pad_block_row, n_used, xn, block_e.shape[0], 256, bm)
        ys = _moe(n_used, block_e, xs, wgu_b, bgu, wdn_b, bdn, bm)
        y = _combine(pos, x1, gates, ys, LANE)
        return y.reshape(batch, n, D_MODEL)

    bias_cache = {}

    return (trunk(x_prompt), trunk(x_sample))
```

```python
import functools
import math

import numpy as np
import jax
import jax.numpy as jnp
from jax import lax
from jax.experimental import pallas as pl
from jax.experimental.pallas import tpu as pltpu

D_MODEL = 1024
N_META = 16
GRID_W = 64
DIFF_HEADS = 8
DIFF_DH = 64
DIFF_VD = 2 * DIFF_DH
NA_HEADS = 16
NA_DH = 64
NA_WIN_H = 8
NA_WIN_W = 16
ROT_DIMS = DIFF_DH // 4
ROPE_THETA = 500000.0
N_EXPERTS = 32
TOP_K = 4
D_FF = D_MODEL
SWIGLU_ALPHA = 1.702
SWIGLU_LIMIT = 7.0
EPS = 1e-6
NEG = -1e30
LAM_INIT = 0.8 - 0.6 * math.exp(-0.3 * 0)
LOG2E = 1.4426950408889634
Q_SCALE = (DIFF_DH ** -0.5) * LOG2E
SCORE_BOUND = 60.0

N_SEG = 8
LANE = 128
NA_ROWS_PER_STEP = 2
NA_KEY_ROWS = NA_WIN_H + NA_ROWS_PER_STEP
VMEM_LIMIT = 56 * 1024 * 1024

F32 = jnp.float32
BF16 = jnp.bfloat16


def _cparams(sem):
    return pltpu.CompilerParams(dimension_semantics=sem, vmem_limit_bytes=VMEM_LIMIT)


def _head_norm_t(y_t, g_ref, cos, sin, scale):
    tm = y_t.shape[1]
    y3 = y_t.reshape(D_MODEL // DIFF_DH, DIFF_DH, tm)
    ms = jnp.mean(y3 * y3, axis=1, keepdims=True)
    y3 = y3 * lax.rsqrt(ms + EPS) * g_ref[...].reshape(1, DIFF_DH, 1)
    if cos is not None:
        half = ROT_DIMS // 2
        x1 = y3[:, 0:half, :]
        x2 = y3[:, half:ROT_DIMS, :]
        c = cos[None]
        s = sin[None]
        y3 = jnp.concatenate([x1 * c - x2 * s, x2 * c + x1 * s, y3[:, ROT_DIMS:, :]], axis=1)
    if scale != 1.0:
        y3 = y3 * scale
    return y3.reshape(D_MODEL, tm)


def _in_proj_kernel(x_ref, g1_ref, w_ref, cos_ref, sin_ref, gqd_ref, gkd_ref, gqn_ref, gkn_ref,
                    qd_t_ref, kd_ref, vd_t_ref, qn_t_ref, kn_ref, vn_t_ref, ga_ref, gb_ref, h_ref):
    j = pl.program_id(1)

    @pl.when(j == 0)
    def _():
        x = x_ref[...]
        ms = jnp.mean(x * x, axis=-1, keepdims=True)
        h_ref[...] = (x * lax.rsqrt(ms + EPS) * g1_ref[...]).astype(BF16)

    y = jnp.dot(h_ref[...], w_ref[...], preferred_element_type=F32)

    @pl.when(j == 0)
    def _():
        qd_t_ref[...] = _head_norm_t(y.T, gqd_ref, cos_ref[...], sin_ref[...], Q_SCALE).astype(BF16)

    @pl.when(j == 1)
    def _():
        kd_ref[...] = _head_norm_t(y.T, gkd_ref, cos_ref[...], sin_ref[...], 1.0).T.astype(BF16)

    @pl.when(j == 2)
    def _():
        vd_t_ref[...] = y.T.astype(BF16)

    @pl.when(j == 3)
    def _():
        qn_t_ref[...] = _head_norm_t(y.T, gqn_ref, None, None, Q_SCALE).astype(BF16)

    @pl.when(j == 4)
    def _():
        kn_ref[...] = _head_norm_t(y.T, gkn_ref, None, None, 1.0).T.astype(BF16)

    @pl.when(j == 5)
    def _():
        vn_t_ref[...] = y.T.astype(BF16)

    @pl.when(j == 6)
    def _():
        ga_ref[...] = jax.nn.sigmoid(y).astype(BF16)

    @pl.when(j == 7)
    def _():
        gb_ref[...] = jax.nn.sigmoid(y).astype(BF16)


def _in_proj(x2d, g1, w_in, cos_t, sin_t, gqd, gkd, gqn, gkn, tm):
    t = x2d.shape[0]
    n_pos_blocks = cos_t.shape[1] // tm
    tok_major = pl.BlockSpec((tm, D_MODEL), lambda i, j: (i, 0))
    feat_major = pl.BlockSpec((D_MODEL, tm), lambda i, j: (0, i))
    small = lambda shape: pl.BlockSpec(shape, lambda i, j: (0, 0))
    tm_shape = jax.ShapeDtypeStruct((t, D_MODEL), BF16)
    fm_shape = jax.ShapeDtypeStruct((D_MODEL, t), BF16)
    return pl.pallas_call(
        _in_proj_kernel,
        grid=(t // tm, N_SEG),
        in_specs=[
            tok_major,
            small((1, D_MODEL)),
            pl.BlockSpec((D_MODEL, D_MODEL), lambda i, j: (0, j)),
            pl.BlockSpec((ROT_DIMS // 2, tm), lambda i, j: (0, i % n_pos_blocks)),
            pl.BlockSpec((ROT_DIMS // 2, tm), lambda i, j: (0, i % n_pos_blocks)),
            small((DIFF_DH, 1)), small((DIFF_DH, 1)), small((NA_DH, 1)), small((NA_DH, 1)),
        ],
        out_specs=[feat_major, tok_major, feat_major, feat_major, tok_major, feat_major,
                   tok_major, tok_major],
        out_shape=[fm_shape, tm_shape, fm_shape, fm_shape, tm_shape, fm_shape, tm_shape, tm_shape],
        scratch_shapes=[pltpu.VMEM((tm, D_MODEL), BF16)],
        compiler_params=_cparams(("parallel", "arbitrary")),
        name="in_proj",
    )(x2d, g1, w_in, cos_t, sin_t, gqd, gkd, gqn, gkn)


def _split_halves(q_t):
    row = lax.broadcasted_iota(jnp.int32, q_t.shape, 0)
    zero = jnp.zeros_like(q_t)
    return jnp.concatenate([jnp.where(row < DIFF_DH, q_t, zero),
                            jnp.where(row >= DIFF_DH, q_t, zero)], axis=1)


def _diff_attn_kernel(bounded_ref, q_t_ref, k_ref, v_t_ref, km_ref, vm_t_ref, lq1_ref, lk1_ref, lq2_ref,
                      lk2_ref, subg_ref, o_ref, acc_ref, l_ref, *, tk):
    tq = q_t_ref.shape[1]
    n = k_ref.shape[0]
    qz = _split_halves(q_t_ref[...])

    @pl.when(bounded_ref[0] != 0)
    def _():
        p = jnp.exp2(jnp.dot(km_ref[...], qz, preferred_element_type=F32))
        l = jnp.sum(p, axis=0, keepdims=True)
        acc = jnp.dot(vm_t_ref[...], p.astype(BF16), preferred_element_type=F32)
        for c in range(n // tk):
            p = jnp.exp2(jnp.dot(k_ref[c * tk:(c + 1) * tk, :], qz, preferred_element_type=F32))
            l = l + jnp.sum(p, axis=0, keepdims=True)
            acc = acc + jnp.dot(v_t_ref[:, c * tk:(c + 1) * tk], p.astype(BF16),
                                preferred_element_type=F32)
        acc_ref[...] = acc
        l_ref[...] = l

    @pl.when(bounded_ref[0] == 0)
    def _():
        s = jnp.dot(km_ref[...], qz, preferred_element_type=F32)
        m = jnp.max(s, axis=0, keepdims=True)
        p = jnp.exp2(s - m)
        l = jnp.sum(p, axis=0, keepdims=True)
        acc = jnp.dot(vm_t_ref[...], p.astype(BF16), preferred_element_type=F32)
        for c in range(n // tk):
            s = jnp.dot(k_ref[c * tk:(c + 1) * tk, :], qz, preferred_element_type=F32)
            m_new = jnp.maximum(m, jnp.max(s, axis=0, keepdims=True))
            alpha = jnp.exp2(m - m_new)
            p = jnp.exp2(s - m_new)
            l = alpha * l + jnp.sum(p, axis=0, keepdims=True)
            acc = alpha * acc + jnp.dot(v_t_ref[:, c * tk:(c + 1) * tk], p.astype(BF16),
                                        preferred_element_type=F32)
            m = m_new
        acc_ref[...] = acc
        l_ref[...] = l

    lam = (jnp.exp(jnp.sum(lq1_ref[...] * lk1_ref[...], axis=-1, keepdims=True))
           - jnp.exp(jnp.sum(lq2_ref[...] * lk2_ref[...], axis=-1, keepdims=True)) + LAM_INIT)
    o = acc_ref[...] / l_ref[...]
    o_t = o[:, :tq] - lam * o[:, tq:]
    ms = jnp.mean(o_t * o_t, axis=0, keepdims=True)
    o_t = o_t * lax.rsqrt(ms + EPS) * subg_ref[...] * (1.0 - LAM_INIT)
    o_ref[...] = o_t.T.astype(BF16)


def _scores_bounded(gq, gk, extra=0.0):
    bound = DIFF_DH * Q_SCALE * jnp.max(jnp.abs(gq)) * jnp.max(jnp.abs(gk)) + extra
    return (bound <= SCORE_BOUND).astype(jnp.int32).reshape(1)


def _diff_attn(bounded, qd_t, kd, vd_t, kd_m, vd_t_m, lq1, lk1, lq2, lk2, subg, batch, n, tq, tk):
    nq = n // tq
    row = lambda shape: pl.BlockSpec(shape, lambda b, h, qi, bd: (0, 0))
    grid_spec = pltpu.PrefetchScalarGridSpec(
        num_scalar_prefetch=1,
        grid=(batch, DIFF_HEADS, nq),
        in_specs=[
            pl.BlockSpec((DIFF_VD, tq), lambda b, h, qi, bd: (h, b * nq + qi)),
            pl.BlockSpec((n, DIFF_VD), lambda b, h, qi, bd: (b, h)),
            pl.BlockSpec((DIFF_VD, n), lambda b, h, qi, bd: (h, b)),
            pl.BlockSpec((N_META, DIFF_VD), lambda b, h, qi, bd: (0, h)),
            pl.BlockSpec((DIFF_VD, N_META), lambda b, h, qi, bd: (h, 0)),
            row((1, DIFF_DH)), row((1, DIFF_DH)), row((1, DIFF_DH)), row((1, DIFF_DH)),
            row((DIFF_VD, 1)),
        ],
        out_specs=pl.BlockSpec((tq, DIFF_VD), lambda b, h, qi, bd: (b * nq + qi, h)),
        scratch_shapes=[pltpu.VMEM((DIFF_VD, 2 * tq), F32), pltpu.VMEM((1, 2 * tq), F32)],
    )
    return pl.pallas_call(
        functools.partial(_diff_attn_kernel, tk=tk),
        grid_spec=grid_spec,
        out_shape=jax.ShapeDtypeStruct((batch * n, D_MODEL), BF16),
        compiler_params=_cparams(("parallel", "parallel", "arbitrary")),
        name="diff_attn",
    )(bounded, qd_t, kd, vd_t, kd_m, vd_t_m, lq1, lk1, lq2, lk2, subg)


def _na_plan(rows):
    wh = min(NA_WIN_H, rows)
    assert wh == NA_WIN_H and rows % NA_ROWS_PER_STEP == 0 and rows >= NA_KEY_ROWS
    start = lambda r: int(np.clip(r - wh // 2, 0, rows - wh))
    sigs, plan = [], []
    for r0 in range(0, rows, NA_ROWS_PER_STEP):
        kr0 = int(np.clip(r0 - wh // 2, 0, rows - NA_KEY_ROWS))
        assert kr0 % 2 == 0
        sig = (r0 - kr0,) + tuple(start(r0 + g) - kr0 for g in range(NA_ROWS_PER_STEP))
        if sig not in sigs:
            sigs.append(sig)
        plan.append((kr0, sigs.index(sig)))
    return plan, sigs


def _na_bias_tables(rpb, meta_bias, sigs):
    nk = NA_KEY_ROWS * GRID_W
    nqr = NA_ROWS_PER_STEP * GRID_W
    kc = np.arange(GRID_W)[:, None]
    qc = np.arange(GRID_W)[None, :]
    cs = np.clip(qc - NA_WIN_W // 2, 0, GRID_W - NA_WIN_W)
    col_ok = (kc >= cs) & (kc < cs + NA_WIN_W)
    dc = np.clip(kc - qc, -(NA_WIN_W - 1), NA_WIN_W - 1) + NA_WIN_W - 1
    toe = jnp.where(col_ok, rpb.astype(F32)[:, :, dc] * LOG2E, NEG)
    neg = jnp.full((NA_HEADS, GRID_W, GRID_W), NEG, F32)
    per_class = []
    for sig in sigs:
        dq, starts = sig[0], sig[1:]
        key_rows = []
        for jr in range(NA_KEY_ROWS):
            blocks = []
            for g in range(NA_ROWS_PER_STEP):
                in_window = starts[g] <= jr < starts[g] + NA_WIN_H
                dr = jr - dq - g + NA_WIN_H - 1
                assert not in_window or 0 <= dr < 2 * NA_WIN_H - 1
                blocks.append(toe[:, dr] if in_window else neg)
            key_rows.append(jnp.concatenate(blocks, axis=-1))
        per_class.append(jnp.concatenate(key_rows, axis=-2))
    bias = jnp.stack(per_class, axis=1)
    c = len(sigs)
    bias = bias.reshape(NA_HEADS // 2, 2, c, nk, nqr).transpose(0, 2, 3, 1, 4)
    bias = bias.reshape(NA_HEADS // 2, c, nk, 2 * nqr)
    mb = (meta_bias.astype(F32) * LOG2E).reshape(NA_HEADS // 2, 2, N_META)
    mb = jnp.broadcast_to(mb.transpose(0, 2, 1)[:, :, :, None], (NA_HEADS // 2, N_META, 2, nqr))
    return bias, mb.reshape(NA_HEADS // 2, N_META, 2 * nqr)


def _na_kernel(q_t_ref, k_ref, v_t_ref, km_ref, vm_t_ref, bias_ref, mbias_ref, o_ref, *, plan):
    nqr = NA_ROWS_PER_STEP * GRID_W
    nk = NA_KEY_ROWS * GRID_W
    km = km_ref[...]
    vm_t = vm_t_ref[...]
    mbias = mbias_ref[0]
    for rp, (kr0, cls) in enumerate(plan):
        q0 = rp * nqr
        k0 = kr0 * GRID_W
        qz = _split_halves(q_t_ref[:, q0:q0 + nqr])
        s = jnp.dot(k_ref[k0:k0 + nk, :], qz, preferred_element_type=F32) + bias_ref[0, cls]
        sm = jnp.dot(km, qz, preferred_element_type=F32) + mbias
        m = jnp.maximum(jnp.max(s, axis=0, keepdims=True), jnp.max(sm, axis=0, keepdims=True))
        p = jnp.exp2(s - m)
        pm = jnp.exp2(sm - m)
        l = jnp.sum(p, axis=0, keepdims=True) + jnp.sum(pm, axis=0, keepdims=True)
        acc = (jnp.dot(v_t_ref[:, k0:k0 + nk], p.astype(BF16), preferred_element_type=F32)
               + jnp.dot(vm_t, pm.astype(BF16), preferred_element_type=F32))
        o = acc / l
        o_t = jnp.concatenate([o[:NA_DH, :nqr], o[NA_DH:, nqr:]], axis=0)
        o_ref[q0:q0 + nqr, :] = o_t.T.astype(BF16)


def _na_attn(qn_t, kn, vn_t, kn_m, vn_t_m, bias, mbias, plan, batch, n):
    n_cls = bias.shape[1]
    hp = NA_HEADS // 2
    return pl.pallas_call(
        functools.partial(_na_kernel, plan=plan),
        grid=(batch, hp),
        in_specs=[
            pl.BlockSpec((2 * NA_DH, n), lambda b, h: (h, b)),
            pl.BlockSpec((n, 2 * NA_DH), lambda b, h: (b, h)),
            pl.BlockSpec((2 * NA_DH, n), lambda b, h: (h, b)),
            pl.BlockSpec((N_META, 2 * NA_DH), lambda b, h: (0, h)),
            pl.BlockSpec((2 * NA_DH, N_META), lambda b, h: (h, 0)),
            pl.BlockSpec((1, n_cls) + bias.shape[2:], lambda b, h: (h, 0, 0, 0)),
            pl.BlockSpec((1,) + mbias.shape[1:], lambda b, h: (h, 0, 0)),
        ],
        out_specs=pl.BlockSpec((n, 2 * NA_DH), lambda b, h: (b, h)),
        out_shape=jax.ShapeDtypeStruct((batch * n, D_MODEL), BF16),
        compiler_params=_cparams(("parallel", "arbitrary")),
        name="na_attn",
    )(qn_t, kn, vn_t, kn_m, vn_t_m, bias, mbias)


def _out_proj_kernel(oa_ref, ob_ref, ga_ref, gb_ref, x_ref, wd_ref, wn_ref, wo_ref, g2_ref, wr_ref,
                     br_ref, x1_ref, xn_ref, topi_ref, gate_ref):
    ya = jnp.dot(oa_ref[...], wd_ref[...], preferred_element_type=F32)
    yb = jnp.dot(ob_ref[...], wn_ref[...], preferred_element_type=F32)
    merged = ga_ref[...].astype(F32) * ya + gb_ref[...].astype(F32) * yb
    x1 = x_ref[...] + jnp.dot(merged.astype(BF16), wo_ref[...], preferred_element_type=F32)
    x1_ref[...] = x1
    ms = jnp.mean(x1 * x1, axis=-1, keepdims=True)
    xn = x1 * lax.rsqrt(ms + EPS) * g2_ref[...]
    xn_ref[...] = xn
    x_hi = xn.astype(BF16)
    x_lo = (xn - x_hi.astype(F32)).astype(BF16)
    hh = jnp.dot(x_hi, wr_ref[...], preferred_element_type=F32)
    lh = jnp.dot(x_lo, wr_ref[:, :N_EXPERTS], preferred_element_type=F32)
    logits = hh[:, :N_EXPERTS] + (hh[:, N_EXPERTS:] + lh) + br_ref[...]
    tm = logits.shape[0]
    lane_e = lax.broadcasted_iota(jnp.int32, logits.shape, 1)
    lane_o = lax.broadcasted_iota(jnp.int32, (tm, LANE), 1)
    topi = jnp.zeros((tm, LANE), jnp.int32)
    topv = jnp.zeros((tm, LANE), F32)
    v0 = None
    den = jnp.zeros((tm, 1), F32)
    for k in range(TOP_K):
        vk = jnp.max(logits, axis=-1, keepdims=True)
        ik = jnp.min(jnp.where(logits == vk, lane_e, N_EXPERTS), axis=-1, keepdims=True)
        logits = jnp.where(lane_e == ik, -jnp.inf, logits)
        if k == 0:
            v0 = vk
        ek = jnp.exp(vk - v0)
        den = den + ek
        topi = jnp.where(lane_o == k, ik, topi)
        topv = jnp.where(lane_o == k, ek, topv)
    topi_ref[...] = topi
    gate_ref[...] = topv / den


def _out_proj(o_a, o_b, ga, gb, x2d, wd, wn, wo, g2, wr, br, tm):
    t = x2d.shape[0]
    tok = pl.BlockSpec((tm, D_MODEL), lambda i: (i, 0))
    full = lambda shape: pl.BlockSpec(shape, lambda i: (0, 0))
    narrow = pl.BlockSpec((tm, LANE), lambda i: (i, 0))
    return pl.pallas_call(
        _out_proj_kernel,
        grid=(t // tm,),
        in_specs=[tok, tok, tok, tok, tok,
                  full((D_MODEL, D_MODEL)), full((D_MODEL, D_MODEL)), full((D_MODEL, D_MODEL)),
                  full((1, D_MODEL)), full((D_MODEL, 2 * N_EXPERTS)), full((1, N_EXPERTS))],
        out_specs=[tok, tok, narrow, narrow],
        out_shape=[jax.ShapeDtypeStruct((t, D_MODEL), F32), jax.ShapeDtypeStruct((t, D_MODEL), F32),
                   jax.ShapeDtypeStruct((t, LANE), jnp.int32), jax.ShapeDtypeStruct((t, LANE), F32)],
        compiler_params=_cparams(("parallel",)),
        name="out_proj",
    )(o_a, o_b, ga, gb, x2d, wd, wn, wo, g2, wr, br)


def _route_plan(topi, t, bm):
    e_ids = jnp.arange(N_EXPERTS, dtype=jnp.int32)
    onehot = topi[:, :, None] == e_ids[None, None, :]
    routed = jnp.sum(onehot, axis=1).astype(jnp.int32)
    incl = jnp.cumsum(routed, axis=0)
    counts = incl[-1]
    padded = (counts + bm - 1) // bm * bm
    pend = jnp.cumsum(padded)
    pstart = pend - padded
    slot_of = (pstart[None, :] + incl - routed)[:, None, :]
    pos = jnp.sum(jnp.where(onehot, slot_of, 0), axis=-1).astype(jnp.int32)
    n_blocks = -(-(t * TOP_K + N_EXPERTS * (bm - 1)) // bm)
    block_first = jnp.arange(n_blocks, dtype=jnp.int32) * bm
    block_e = jnp.minimum(jnp.sum((block_first[:, None] >= pend[None, :]).astype(jnp.int32), axis=1),
                          N_EXPERTS - 1)
    n_used = (pend[-1:] // bm).astype(jnp.int32)
    pad_block_row = jnp.where(padded > 0, pend - bm, -1).astype(jnp.int32)
    return pos.reshape(-1), block_e, n_used, pad_block_row


def _dispatch_kernel(pos_ref, padrow_ref, nused_ref, xn_ref, xs_hbm, zbuf, zsem, sem, *, tm, bm, n_blocks):
    i = pl.program_id(0)

    def zero_copy(row0):
        return pltpu.make_async_copy(zbuf, xs_hbm.at[pl.ds(pl.multiple_of(row0, bm), bm)], zsem)

    def for_each_zero_block(fn):
        for e in range(N_EXPERTS):
            @pl.when(padrow_ref[e] >= 0)
            def _():
                fn(zero_copy(padrow_ref[e]))

        def tail(b, c):
            fn(zero_copy(b * bm))
            return c
        lax.fori_loop(nused_ref[0], n_blocks, tail, 0)

    @pl.when(i == 0)
    def _():
        zbuf[...] = jnp.zeros_like(zbuf)
        for_each_zero_block(lambda cp: cp.start())
        for_each_zero_block(lambda cp: cp.wait())

    def row_copy(r, k):
        return pltpu.make_async_copy(xn_ref.at[pl.ds(r, 1)],
                                     xs_hbm.at[pl.ds(pos_ref[(i * tm + r) * TOP_K + k], 1)], sem)

    def for_each_row(fn):
        def body(r, c):
            for k in range(TOP_K):
                fn(row_copy(r, k))
            return c
        lax.fori_loop(0, tm, body, 0, unroll=4)

    for_each_row(lambda cp: cp.start())
    for_each_row(lambda cp: cp.wait())


def _dispatch(pos, pad_block_row, n_used, xn, n_blocks, tm, bm):
    t = xn.shape[0]
    grid_spec = pltpu.PrefetchScalarGridSpec(
        num_scalar_prefetch=3,
        grid=(t // tm,),
        in_specs=[pl.BlockSpec((tm, D_MODEL), lambda i, ps, pr, nu: (i, 0))],
        out_specs=pl.BlockSpec(memory_space=pl.ANY),
        scratch_shapes=[pltpu.VMEM((bm, D_MODEL), F32), pltpu.SemaphoreType.DMA(()),
                        pltpu.SemaphoreType.DMA(())],
    )
    return pl.pallas_call(
        functools.partial(_dispatch_kernel, tm=tm, bm=bm, n_blocks=n_blocks),
        grid_spec=grid_spec,
        out_shape=jax.ShapeDtypeStruct((n_blocks * bm, D_MODEL), F32),
        compiler_params=_cparams(("arbitrary",)),
        name="moe_dispatch",
    )(pos, pad_block_row, n_used, xn)


def _moe_kernel(nused_ref, be_ref, xs_ref, wgu_ref, bgu_ref, wd_ref, bd_ref, ys_ref):
    i = pl.program_id(0)

    @pl.when(i < nused_ref[0])
    def _():
        x = xs_ref[...].astype(BF16)
        gu = jnp.dot(x, wgu_ref[0], preferred_element_type=F32) + bgu_ref[0]
        gate = jnp.minimum(gu[:, :D_FF], SWIGLU_LIMIT)
        up = jnp.clip(gu[:, D_FF:], -SWIGLU_LIMIT, SWIGLU_LIMIT)
        act = gate * jax.nn.sigmoid(SWIGLU_ALPHA * gate) * (up + 1.0)
        ys_ref[...] = jnp.dot(act.astype(BF16), wd_ref[0], preferred_element_type=F32) + bd_ref[0]

    @pl.when(i >= nused_ref[0])
    def _():
        ys_ref[...] = jnp.zeros_like(ys_ref)


def _moe(n_used, block_e, xs, wgu, bgu, wd, bd, bm):
    n_blocks = block_e.shape[0]
    grid_spec = pltpu.PrefetchScalarGridSpec(
        num_scalar_prefetch=2,
        grid=(n_blocks,),
        in_specs=[
            pl.BlockSpec((bm, D_MODEL), lambda i, nu, be: (i, 0)),
            pl.BlockSpec((1, D_MODEL, 2 * D_FF), lambda i, nu, be: (be[i], 0, 0)),
            pl.BlockSpec((1, 1, 2 * D_FF), lambda i, nu, be: (be[i], 0, 0)),
            pl.BlockSpec((1, D_FF, D_MODEL), lambda i, nu, be: (be[i], 0, 0)),
            pl.BlockSpec((1, 1, D_MODEL), lambda i, nu, be: (be[i], 0, 0)),
        ],
        out_specs=pl.BlockSpec((bm, D_MODEL), lambda i, nu, be: (i, 0)),
    )
    return pl.pallas_call(
        _moe_kernel,
        grid_spec=grid_spec,
        out_shape=jax.ShapeDtypeStruct((n_blocks * bm, D_MODEL), F32),
        compiler_params=_cparams(("arbitrary",)),
        name="moe_experts",
    )(n_used, block_e, xs, wgu, bgu, wd, bd)


def _combine_kernel(pos_ref, x1_ref, gate_ref, ys_hbm, o_ref, buf, sem, *, tm):
    i = pl.program_id(0)
    n_steps = pl.num_programs(0)
    slot = i % 2

    def row_copy(step, s, r, k):
        tok = step * tm + r
        return pltpu.make_async_copy(ys_hbm.at[pl.ds(pos_ref[tok * TOP_K + k], 1)],
                                     buf.at[s, k, pl.ds(r, 1)], sem.at[s])

    def for_each_row(step, s, fn):
        def body(r, c):
            for k in range(TOP_K):
                fn(row_copy(step, s, r, k))
            return c
        lax.fori_loop(0, tm, body, 0, unroll=4)

    @pl.when(i == 0)
    def _():
        for_each_row(0, 0, lambda cp: cp.start())

    @pl.when(i + 1 < n_steps)
    def _():
        for_each_row(i + 1, 1 - slot, lambda cp: cp.start())

    for_each_row(i, slot, lambda cp: cp.wait())
    g = gate_ref[...]
    acc = x1_ref[...]
    for k in range(TOP_K):
        acc = acc + g[:, k:k + 1] * buf[slot, k]
    o_ref[...] = acc


def _combine(pos, x1, gates, ys, tm):
    t = x1.shape[0]
    grid_spec = pltpu.PrefetchScalarGridSpec(
        num_scalar_prefetch=1,
        grid=(t // tm,),
        in_specs=[pl.BlockSpec((tm, D_MODEL), lambda i, ps: (i, 0)),
                  pl.BlockSpec((tm, LANE), lambda i, ps: (i, 0)),
                  pl.BlockSpec(memory_space=pl.ANY)],
        out_specs=pl.BlockSpec((tm, D_MODEL), lambda i, ps: (i, 0)),
        scratch_shapes=[pltpu.VMEM((2, TOP_K, tm, D_MODEL), F32), pltpu.SemaphoreType.DMA((2,))],
    )
    return pl.pallas_call(
        functools.partial(_combine_kernel, tm=tm),
        grid_spec=grid_spec,
        out_shape=jax.ShapeDtypeStruct((t, D_MODEL), F32),
        compiler_params=_cparams(("arbitrary",)),
        name="moe_combine",
    )(pos, x1, gates, ys)


def _rope_tables_t(first_pos, count):
    inv = ROPE_THETA ** (-jnp.arange(0, ROT_DIMS, 2, dtype=F32) / ROT_DIMS)
    ang = (first_pos + jnp.arange(count, dtype=F32))[:, None] * inv[None, :]
    return jnp.cos(ang).T, jnp.sin(ang).T


def _pick(n, candidates):
    for c in candidates:
        if n % c == 0:
            return c
    raise ValueError(f"no tile for {n}")


def kernel(x_prompt, x_sample, meta_tokens, norm1_g, w_in, diff_q_g, diff_k_g, lam_q1, lam_k1, lam_q2, lam_k2, diff_sub_g, w_diff_out, na_q_g, na_k_g, na_rpb, na_meta_bias, w_na_out, w_o, norm2_g, w_router, b_router, w_gate_up, b_gate_up, w_down, b_down):
    l = 0
    col = lambda v: v.astype(F32).reshape(-1, 1)
    rowv = lambda v: v.astype(F32).reshape(1, -1)
    g1 = rowv(norm1_g[l])
    w_in_b = w_in[l].astype(BF16)
    gqd, gkd, gqn, gkn = col(diff_q_g[l]), col(diff_k_g[l]), col(na_q_g[l]), col(na_k_g[l])
    lq1, lk1, lq2, lk2 = rowv(lam_q1[l]), rowv(lam_k1[l]), rowv(lam_q2[l]), rowv(lam_k2[l])
    subg = col(diff_sub_g[l])
    wd_b, wn_b, wo_b = w_diff_out[l].astype(BF16), w_na_out[l].astype(BF16), w_o[l].astype(BF16)
    g2 = rowv(norm2_g[l])
    wr_hi = w_router[l].astype(BF16)
    wr_lo = (w_router[l].astype(F32) - wr_hi.astype(F32)).astype(BF16)
    wr = jnp.concatenate([wr_hi, wr_lo], axis=1)
    br = rowv(b_router[l])
    diff_bounded = _scores_bounded(diff_q_g[l], diff_k_g[l])
    wgu_b = w_gate_up[l].astype(BF16)
    bgu = b_gate_up[l].astype(F32).reshape(N_EXPERTS, 1, 2 * D_FF)
    wdn_b = w_down[l].astype(BF16)
    bdn = b_down[l].astype(F32).reshape(N_EXPERTS, 1, D_MODEL)

    meta_pad = jnp.zeros((LANE, D_MODEL), F32).at[:N_META].set(meta_tokens.astype(F32))
    cos_m, sin_m = _rope_tables_t(0.0, LANE)
    _, kd_m, vd_t_m, _, kn_m, vn_t_m, _, _ = _in_proj(meta_pad, g1, w_in_b, cos_m, sin_m,
                                                      gqd, gkd, gqn, gkn, LANE)
    kd_m, kn_m = kd_m[:N_META], kn_m[:N_META]
    vd_t_m, vn_t_m = vd_t_m[:, :N_META], vn_t_m[:, :N_META]

    def trunk(x):
        batch, n, _ = x.shape
        t = batch * n
        x2d = x.reshape(t, D_MODEL)
        tm = _pick(n, (512, 256, 128))
        cos_t, sin_t = _rope_tables_t(float(N_META), n)
        qd_t, kd, vd_t, qn_t, kn, vn_t, ga, gb = _in_proj(x2d, g1, w_in_b, cos_t, sin_t,
                                                         gqd, gkd, gqn, gkn, tm)
        tq = _pick(n, (512, 256, 128))
        o_a = _diff_attn(diff_bounded, qd_t, kd, vd_t, kd_m, vd_t_m, lq1, lk1, lq2, lk2, subg,
                         batch, n, tq, tq)
        plan, sigs = _na_plan(n // GRID_W)
        if tuple(sigs) not in bias_cache:
            bias_cache[tuple(sigs)] = _na_bias_tables(na_rpb[l], na_meta_bias[l], sigs)
        bias, mbias = bias_cache[tuple(sigs)]
        o_b = _na_attn(qn_t, kn, vn_t, kn_m, vn_t_m, bias, mbias, plan, batch, n)
        x1, xn, topi, gates = _out_proj(o_a, o_b, ga, gb, x2d, wd_b, wn_b, wo_b, g2, wr, br, tm)
        bm = 256
        pos, block_e, n_used, pad_block_row = _route_plan(topi[:, :TOP_K], t, bm)
        xs = _dispatch(pos, pad_block_row, n_used, xn, block_e.shape[0], 256, bm)
        ys = _moe(n_used, block_e, xs, wgu_b, bgu, wdn_b, bdn, bm)
        y = _combine(pos, x1, gates, ys, LANE)
        return y.reshape(batch, n, D_MODEL)

    bias_cache = {}

    return (trunk(x_prompt), trunk(x_sample))
```

```python
import functools
import math

import numpy as np
import jax
import jax.numpy as jnp
from jax import lax
from jax.experimental import pallas as pl
from jax.experimental.pallas import tpu as pltpu

D_MODEL = 1024
N_META = 16
GRID_W = 64
DIFF_HEADS = 8
DIFF_DH = 64
DIFF_VD = 2 * DIFF_DH
NA_HEADS = 16
NA_DH = 64
NA_WIN_H = 8
NA_WIN_W = 16
ROT_DIMS = DIFF_DH // 4
ROPE_THETA = 500000.0
N_EXPERTS = 32
TOP_K = 4
D_FF = D_MODEL
SWIGLU_ALPHA = 1.702
SWIGLU_LIMIT = 7.0
EPS = 1e-6
NEG = -1e30
LAM_INIT = 0.8 - 0.6 * math.exp(-0.3 * 0)
LOG2E = 1.4426950408889634
Q_SCALE = (DIFF_DH ** -0.5) * LOG2E
SCORE_BOUND = 60.0

N_SEG = 8
LANE = 128
NA_ROWS_PER_STEP = 2
NA_KEY_ROWS = NA_WIN_H + NA_ROWS_PER_STEP
VMEM_LIMIT = 56 * 1024 * 1024

F32 = jnp.float32
BF16 = jnp.bfloat16


def _cparams(sem):
    return pltpu.CompilerParams(dimension_semantics=sem, vmem_limit_bytes=VMEM_LIMIT)


def _head_norm_t(y_t, g_ref, cos, sin, scale):
    tm = y_t.shape[1]
    y3 = y_t.reshape(D_MODEL // DIFF_DH, DIFF_DH, tm)
    ms = jnp.mean(y3 * y3, axis=1, keepdims=True)
    y3 = y3 * lax.rsqrt(ms + EPS) * g_ref[...].reshape(1, DIFF_DH, 1)
    if cos is not None:
        half = ROT_DIMS // 2
        x1 = y3[:, 0:half, :]
        x2 = y3[:, half:ROT_DIMS, :]
        c = cos[None]
        s = sin[None]
        y3 = jnp.concatenate([x1 * c - x2 * s, x2 * c + x1 * s, y3[:, ROT_DIMS:, :]], axis=1)
    if scale != 1.0:
        y3 = y3 * scale
    return y3.reshape(D_MODEL, tm)


def _in_proj_kernel(x_ref, g1_ref, wt_ref, wg_ref, cos_ref, sin_ref, gqd_ref, gkd_ref, gqn_ref, gkn_ref,
                    qd_t_ref, kd_ref, vd_t_ref, qn_t_ref, kn_ref, vn_t_ref, ga_ref, gb_ref,
                    h_ref, ht_ref):
    j = pl.program_id(1)

    @pl.when(j == 0)
    def _():
        x = x_ref[...]
        ms = jnp.mean(x * x, axis=-1, keepdims=True)
        h = x * lax.rsqrt(ms + EPS) * g1_ref[...]
        h_ref[...] = h.astype(BF16)
        ht_ref[...] = h.T.astype(BF16)

    def proj_t():
        return jnp.dot(wt_ref[...], ht_ref[...], preferred_element_type=F32)

    def proj():
        return jnp.dot(h_ref[...], wg_ref[...], preferred_element_type=F32)

    @pl.when(j == 0)
    def _():
        qd_t_ref[...] = _head_norm_t(proj_t(), gqd_ref, cos_ref[...], sin_ref[...], Q_SCALE).astype(BF16)

    @pl.when(j == 1)
    def _():
        kd_ref[...] = _head_norm_t(proj_t(), gkd_ref, cos_ref[...], sin_ref[...], 1.0).T.astype(BF16)

    @pl.when(j == 2)
    def _():
        vd_t_ref[...] = proj_t().astype(BF16)

    @pl.when(j == 3)
    def _():
        qn_t_ref[...] = _head_norm_t(proj_t(), gqn_ref, None, None, Q_SCALE).astype(BF16)

    @pl.when(j == 4)
    def _():
        kn_ref[...] = _head_norm_t(proj_t(), gkn_ref, None, None, 1.0).T.astype(BF16)

    @pl.when(j == 5)
    def _():
        vn_t_ref[...] = proj_t().astype(BF16)

    @pl.when(j == 6)
    def _():
        ga_ref[...] = jax.nn.sigmoid(proj()).astype(BF16)

    @pl.when(j == 7)
    def _():
        gb_ref[...] = jax.nn.sigmoid(proj()).astype(BF16)


N_FM_SEG = 6


def _in_proj(x2d, g1, w_t, w_g, cos_t, sin_t, gqd, gkd, gqn, gkn, tm):
    t = x2d.shape[0]
    n_pos_blocks = cos_t.shape[1] // tm
    tok_major = pl.BlockSpec((tm, D_MODEL), lambda i, j: (i, 0))
    feat_major = pl.BlockSpec((D_MODEL, tm), lambda i, j: (0, i))
    small = lambda shape: pl.BlockSpec(shape, lambda i, j: (0, 0))
    tm_shape = jax.ShapeDtypeStruct((t, D_MODEL), BF16)
    fm_shape = jax.ShapeDtypeStruct((D_MODEL, t), BF16)
    return pl.pallas_call(
        _in_proj_kernel,
        grid=(t // tm, N_SEG),
        in_specs=[
            tok_major,
            small((1, D_MODEL)),
            pl.BlockSpec((D_MODEL, D_MODEL), lambda i, j: (jnp.minimum(j, N_FM_SEG - 1), 0)),
            pl.BlockSpec((D_MODEL, D_MODEL), lambda i, j: (0, jnp.maximum(j - N_FM_SEG, 0))),
            pl.BlockSpec((ROT_DIMS // 2, tm), lambda i, j: (0, i % n_pos_blocks)),
            pl.BlockSpec((ROT_DIMS // 2, tm), lambda i, j: (0, i % n_pos_blocks)),
            small((DIFF_DH, 1)), small((DIFF_DH, 1)), small((NA_DH, 1)), small((NA_DH, 1)),
        ],
        out_specs=[feat_major, tok_major, feat_major, feat_major, tok_major, feat_major,
                   tok_major, tok_major],
        out_shape=[fm_shape, tm_shape, fm_shape, fm_shape, tm_shape, fm_shape, tm_shape, tm_shape],
        scratch_shapes=[pltpu.VMEM((tm, D_MODEL), BF16), pltpu.VMEM((D_MODEL, tm), BF16)],
        compiler_params=_cparams(("parallel", "arbitrary")),
        name="in_proj",
    )(x2d, g1, w_t, w_g, cos_t, sin_t, gqd, gkd, gqn, gkn)


def _split_halves(q_t):
    row = lax.broadcasted_iota(jnp.int32, q_t.shape, 0)
    zero = jnp.zeros_like(q_t)
    return jnp.concatenate([jnp.where(row < DIFF_DH, q_t, zero),
                            jnp.where(row >= DIFF_DH, q_t, zero)], axis=1)


def _diff_attn_kernel(bounded_ref, q_t_ref, k_ref, v_t_ref, km_ref, vm_t_ref, lq1_ref, lk1_ref, lq2_ref,
                      lk2_ref, subg_ref, o_ref, acc_ref, l_ref, *, tk):
    tq = q_t_ref.shape[1]
    n = k_ref.shape[0]
    qz = _split_halves(q_t_ref[...])

    @pl.when(bounded_ref[0] != 0)
    def _():
        p = jnp.exp2(jnp.dot(km_ref[...], qz, preferred_element_type=F32))
        l = jnp.sum(p, axis=0, keepdims=True)
        acc = jnp.dot(vm_t_ref[...], p.astype(BF16), preferred_element_type=F32)
        for c in range(n // tk):
            p = jnp.exp2(jnp.dot(k_ref[c * tk:(c + 1) * tk, :], qz, preferred_element_type=F32))
            l = l + jnp.sum(p, axis=0, keepdims=True)
            acc = acc + jnp.dot(v_t_ref[:, c * tk:(c + 1) * tk], p.astype(BF16),
                                preferred_element_type=F32)
        acc_ref[...] = acc
        l_ref[...] = l

    @pl.when(bounded_ref[0] == 0)
    def _():
        s = jnp.dot(km_ref[...], qz, preferred_element_type=F32)
        m = jnp.max(s, axis=0, keepdims=True)
        p = jnp.exp2(s - m)
        l = jnp.sum(p, axis=0, keepdims=True)
        acc = jnp.dot(vm_t_ref[...], p.astype(BF16), preferred_element_type=F32)
        for c in range(n // tk):
            s = jnp.dot(k_ref[c * tk:(c + 1) * tk, :], qz, preferred_element_type=F32)
            m_new = jnp.maximum(m, jnp.max(s, axis=0, keepdims=True))
            alpha = jnp.exp2(m - m_new)
            p = jnp.exp2(s - m_new)
            l = alpha * l + jnp.sum(p, axis=0, keepdims=True)
            acc = alpha * acc + jnp.dot(v_t_ref[:, c * tk:(c + 1) * tk], p.astype(BF16),
                                        preferred_element_type=F32)
            m = m_new
        acc_ref[...] = acc
        l_ref[...] = l

    lam = (jnp.exp(jnp.sum(lq1_ref[...] * lk1_ref[...], axis=-1, keepdims=True))
           - jnp.exp(jnp.sum(lq2_ref[...] * lk2_ref[...], axis=-1, keepdims=True)) + LAM_INIT)
    o = acc_ref[...] / l_ref[...]
    o_t = o[:, :tq] - lam * o[:, tq:]
    ms = jnp.mean(o_t * o_t, axis=0, keepdims=True)
    o_t = o_t * lax.rsqrt(ms + EPS) * subg_ref[...] * (1.0 - LAM_INIT)
    o_ref[...] = o_t.T.astype(BF16)


def _scores_bounded(gq, gk, extra=0.0):
    bound = DIFF_DH * Q_SCALE * jnp.max(jnp.abs(gq)) * jnp.max(jnp.abs(gk)) + extra
    return (bound <= SCORE_BOUND).astype(jnp.int32).reshape(1)


def _diff_attn(bounded, qd_t, kd, vd_t, kd_m, vd_t_m, lq1, lk1, lq2, lk2, subg, batch, n, tq, tk):
    nq = n // tq
    row = lambda shape: pl.BlockSpec(shape, lambda b, h, qi, bd: (0, 0))
    grid_spec = pltpu.PrefetchScalarGridSpec(
        num_scalar_prefetch=1,
        grid=(batch, DIFF_HEADS, nq),
        in_specs=[
            pl.BlockSpec((DIFF_VD, tq), lambda b, h, qi, bd: (h, b * nq + qi)),
            pl.BlockSpec((n, DIFF_VD), lambda b, h, qi, bd: (b, h)),
            pl.BlockSpec((DIFF_VD, n), lambda b, h, qi, bd: (h, b)),
            pl.BlockSpec((N_META, DIFF_VD), lambda b, h, qi, bd: (0, h)),
            pl.BlockSpec((DIFF_VD, N_META), lambda b, h, qi, bd: (h, 0)),
            row((1, DIFF_DH)), row((1, DIFF_DH)), row((1, DIFF_DH)), row((1, DIFF_DH)),
            row((DIFF_VD, 1)),
        ],
        out_specs=pl.BlockSpec((tq, DIFF_VD), lambda b, h, qi, bd: (b * nq + qi, h)),
        scratch_shapes=[pltpu.VMEM((DIFF_VD, 2 * tq), F32), pltpu.VMEM((1, 2 * tq), F32)],
    )
    return pl.pallas_call(
        functools.partial(_diff_attn_kernel, tk=tk),
        grid_spec=grid_spec,
        out_shape=jax.ShapeDtypeStruct((batch * n, D_MODEL), BF16),
        compiler_params=_cparams(("parallel", "parallel", "arbitrary")),
        name="diff_attn",
    )(bounded, qd_t, kd, vd_t, kd_m, vd_t_m, lq1, lk1, lq2, lk2, subg)


def _na_plan(rows):
    wh = min(NA_WIN_H, rows)
    assert wh == NA_WIN_H and rows % NA_ROWS_PER_STEP == 0 and rows >= NA_KEY_ROWS
    start = lambda r: int(np.clip(r - wh // 2, 0, rows - wh))
    sigs, plan = [], []
    for r0 in range(0, rows, NA_ROWS_PER_STEP):
        kr0 = int(np.clip(r0 - wh // 2, 0, rows - NA_KEY_ROWS))
        assert kr0 % 2 == 0
        sig = (r0 - kr0,) + tuple(start(r0 + g) - kr0 for g in range(NA_ROWS_PER_STEP))
        if sig not in sigs:
            sigs.append(sig)
        plan.append((kr0, sigs.index(sig)))
    return plan, sigs


def _na_bias_tables(rpb, meta_bias, sigs):
    nk = NA_KEY_ROWS * GRID_W
    nqr = NA_ROWS_PER_STEP * GRID_W
    kc = np.arange(GRID_W)[:, None]
    qc = np.arange(GRID_W)[None, :]
    cs = np.clip(qc - NA_WIN_W // 2, 0, GRID_W - NA_WIN_W)
    col_ok = (kc >= cs) & (kc < cs + NA_WIN_W)
    dc = np.clip(kc - qc, -(NA_WIN_W - 1), NA_WIN_W - 1) + NA_WIN_W - 1
    toe = jnp.where(col_ok, rpb.astype(F32)[:, :, dc] * LOG2E, NEG)
    neg = jnp.full((NA_HEADS, GRID_W, GRID_W), NEG, F32)
    per_class = []
    for sig in sigs:
        dq, starts = sig[0], sig[1:]
        key_rows = []
        for jr in range(NA_KEY_ROWS):
            blocks = []
            for g in range(NA_ROWS_PER_STEP):
                in_window = starts[g] <= jr < starts[g] + NA_WIN_H
                dr = jr - dq - g + NA_WIN_H - 1
                assert not in_window or 0 <= dr < 2 * NA_WIN_H - 1
                blocks.append(toe[:, dr] if in_window else neg)
            key_rows.append(jnp.concatenate(blocks, axis=-1))
        per_class.append(jnp.concatenate(key_rows, axis=-2))
    bias = jnp.stack(per_class, axis=1)
    c = len(sigs)
    bias = bias.reshape(NA_HEADS // 2, 2, c, nk, nqr).transpose(0, 2, 3, 1, 4)
    bias = bias.reshape(NA_HEADS // 2, c, nk, 2 * nqr)
    mb = (meta_bias.astype(F32) * LOG2E).reshape(NA_HEADS // 2, 2, N_META)
    mb = jnp.broadcast_to(mb.transpose(0, 2, 1)[:, :, :, None], (NA_HEADS // 2, N_META, 2, nqr))
    return bias, mb.reshape(NA_HEADS // 2, N_META, 2 * nqr)


def _na_kernel(bounded_ref, q_t_ref, k_ref, v_t_ref, km_ref, vm_t_ref, bias_ref, mbias_ref, o_ref, *, plan):
    nqr = NA_ROWS_PER_STEP * GRID_W
    nk = NA_KEY_ROWS * GRID_W

    def all_rows(shifted):
        km = km_ref[...]
        vm_t = vm_t_ref[...]
        mbias = mbias_ref[0]
        for rp, (kr0, cls) in enumerate(plan):
            q0 = rp * nqr
            k0 = kr0 * GRID_W
            qz = _split_halves(q_t_ref[:, q0:q0 + nqr])
            s = jnp.dot(k_ref[k0:k0 + nk, :], qz, preferred_element_type=F32) + bias_ref[0, cls]
            sm = jnp.dot(km, qz, preferred_element_type=F32) + mbias
            if shifted:
                m = jnp.maximum(jnp.max(s, axis=0, keepdims=True), jnp.max(sm, axis=0, keepdims=True))
                s = s - m
                sm = sm - m
            p = jnp.exp2(s)
            pm = jnp.exp2(sm)
            l = jnp.sum(p, axis=0, keepdims=True) + jnp.sum(pm, axis=0, keepdims=True)
            acc = (jnp.dot(v_t_ref[:, k0:k0 + nk], p.astype(BF16), preferred_element_type=F32)
                   + jnp.dot(vm_t, pm.astype(BF16), preferred_element_type=F32))
            o = acc / l
            o_t = jnp.concatenate([o[:NA_DH, :nqr], o[NA_DH:, nqr:]], axis=0)
            o_ref[q0:q0 + nqr, :] = o_t.T.astype(BF16)

    @pl.when(bounded_ref[0] != 0)
    def _():
        all_rows(shifted=False)

    @pl.when(bounded_ref[0] == 0)
    def _():
        all_rows(shifted=True)


def _na_attn(bounded, qn_t, kn, vn_t, kn_m, vn_t_m, bias, mbias, plan, batch, n):
    n_cls = bias.shape[1]
    hp = NA_HEADS // 2
    grid_spec = pltpu.PrefetchScalarGridSpec(
        num_scalar_prefetch=1,
        grid=(batch, hp),
        in_specs=[
            pl.BlockSpec((2 * NA_DH, n), lambda b, h, bd: (h, b)),
            pl.BlockSpec((n, 2 * NA_DH), lambda b, h, bd: (b, h)),
            pl.BlockSpec((2 * NA_DH, n), lambda b, h, bd: (h, b)),
            pl.BlockSpec((N_META, 2 * NA_DH), lambda b, h, bd: (0, h)),
            pl.BlockSpec((2 * NA_DH, N_META), lambda b, h, bd: (h, 0)),
            pl.BlockSpec((1, n_cls) + bias.shape[2:], lambda b, h, bd: (h, 0, 0, 0)),
            pl.BlockSpec((1,) + mbias.shape[1:], lambda b, h, bd: (h, 0, 0)),
        ],
        out_specs=pl.BlockSpec((n, 2 * NA_DH), lambda b, h, bd: (b, h)),
    )
    return pl.pallas_call(
        functools.partial(_na_kernel, plan=plan),
        grid_spec=grid_spec,
        out_shape=jax.ShapeDtypeStruct((batch * n, D_MODEL), BF16),
        compiler_params=_cparams(("parallel", "arbitrary")),
        name="na_attn",
    )(bounded, qn_t, kn, vn_t, kn_m, vn_t_m, bias, mbias)


def _out_proj_kernel(oa_ref, ob_ref, ga_ref, gb_ref, x_ref, wd_ref, wn_ref, wo_ref, g2_ref, wr_ref,
                     br_ref, x1_ref, xn_ref, topi_ref, gate_ref):
    ya = jnp.dot(oa_ref[...], wd_ref[...], preferred_element_type=F32)
    yb = jnp.dot(ob_ref[...], wn_ref[...], preferred_element_type=F32)
    merged = ga_ref[...].astype(F32) * ya + gb_ref[...].astype(F32) * yb
    x1 = x_ref[...] + jnp.dot(merged.astype(BF16), wo_ref[...], preferred_element_type=F32)
    x1_ref[...] = x1
    ms = jnp.mean(x1 * x1, axis=-1, keepdims=True)
    xn = x1 * lax.rsqrt(ms + EPS) * g2_ref[...]
    xn_ref[...] = xn
    x_hi = xn.astype(BF16)
    x_lo = (xn - x_hi.astype(F32)).astype(BF16)
    hh = jnp.dot(x_hi, wr_ref[...], preferred_element_type=F32)
    lh = jnp.dot(x_lo, wr_ref[:, :N_EXPERTS], preferred_element_type=F32)
    logits = hh[:, :N_EXPERTS] + (hh[:, N_EXPERTS:] + lh) + br_ref[...]
    tm = logits.shape[0]
    lane_e = lax.broadcasted_iota(jnp.int32, logits.shape, 1)
    lane_o = lax.broadcasted_iota(jnp.int32, (tm, LANE), 1)
    topi = jnp.zeros((tm, LANE), jnp.int32)
    topv = jnp.zeros((tm, LANE), F32)
    v0 = None
    den = jnp.zeros((tm, 1), F32)
    for k in range(TOP_K):
        vk = jnp.max(logits, axis=-1, keepdims=True)
        ik = jnp.min(jnp.where(logits == vk, lane_e, N_EXPERTS), axis=-1, keepdims=True)
        logits = jnp.where(lane_e == ik, -jnp.inf, logits)
        if k == 0:
            v0 = vk
        ek = jnp.exp(vk - v0)
        den = den + ek
        topi = jnp.where(lane_o == k, ik, topi)
        topv = jnp.where(lane_o == k, ek, topv)
    topi_ref[...] = topi
    gate_ref[...] = topv / den


def _out_proj(o_a, o_b, ga, gb, x2d, wd, wn, wo, g2, wr, br, tm):
    t = x2d.shape[0]
    tok = pl.BlockSpec((tm, D_MODEL), lambda i: (i, 0))
    full = lambda shape: pl.BlockSpec(shape, lambda i: (0, 0))
    narrow = pl.BlockSpec((tm, LANE), lambda i: (i, 0))
    return pl.pallas_call(
        _out_proj_kernel,
        grid=(t // tm,),
        in_specs=[tok, tok, tok, tok, tok,
                  full((D_MODEL, D_MODEL)), full((D_MODEL, D_MODEL)), full((D_MODEL, D_MODEL)),
                  full((1, D_MODEL)), full((D_MODEL, 2 * N_EXPERTS)), full((1, N_EXPERTS))],
        out_specs=[tok, tok, narrow, narrow],
        out_shape=[jax.ShapeDtypeStruct((t, D_MODEL), F32), jax.ShapeDtypeStruct((t, D_MODEL), F32),
                   jax.ShapeDtypeStruct((t, LANE), jnp.int32), jax.ShapeDtypeStruct((t, LANE), F32)],
        compiler_params=_cparams(("parallel",)),
        name="out_proj",
    )(o_a, o_b, ga, gb, x2d, wd, wn, wo, g2, wr, br)


def _route_plan(topi, t, bm):
    e_ids = jnp.arange(N_EXPERTS, dtype=jnp.int32)
    onehot = topi[:, :, None] == e_ids[None, None, :]
    routed = jnp.sum(onehot, axis=1).astype(jnp.int32)
    incl = jnp.cumsum(routed, axis=0)
    counts = incl[-1]
    padded = (counts + bm - 1) // bm * bm
    pend = jnp.cumsum(padded)
    pstart = pend - padded
    slot_of = (pstart[None, :] + incl - routed)[:, None, :]
    pos = jnp.sum(jnp.where(onehot, slot_of, 0), axis=-1).astype(jnp.int32)
    n_blocks = -(-(t * TOP_K + N_EXPERTS * (bm - 1)) // bm)
    block_first = jnp.arange(n_blocks, dtype=jnp.int32) * bm
    block_e = jnp.minimum(jnp.sum((block_first[:, None] >= pend[None, :]).astype(jnp.int32), axis=1),
                          N_EXPERTS - 1)
    n_used = (pend[-1:] // bm).astype(jnp.int32)
    pad_block_row = jnp.where(padded > 0, pend - bm, -1).astype(jnp.int32)
    return pos.reshape(-1), block_e, n_used, pad_block_row


def _dispatch_kernel(pos_ref, padrow_ref, nused_ref, xn_ref, xs_hbm, zbuf, zsem, sem, *, tm, bm, n_blocks):
    i = pl.program_id(0)

    def zero_copy(row0):
        return pltpu.make_async_copy(zbuf, xs_hbm.at[pl.ds(pl.multiple_of(row0, bm), bm)], zsem)

    def for_each_zero_block(fn):
        for e in range(N_EXPERTS):
            @pl.when(padrow_ref[e] >= 0)
            def _():
                fn(zero_copy(padrow_ref[e]))

        def tail(b, c):
            fn(zero_copy(b * bm))
            return c
        lax.fori_loop(nused_ref[0], n_blocks, tail, 0)

    @pl.when(i == 0)
    def _():
        zbuf[...] = jnp.zeros_like(zbuf)
        for_each_zero_block(lambda cp: cp.start())
        for_each_zero_block(lambda cp: cp.wait())

    def row_copy(r, k):
        return pltpu.make_async_copy(xn_ref.at[pl.ds(r, 1)],
                                     xs_hbm.at[pl.ds(pos_ref[(i * tm + r) * TOP_K + k], 1)], sem)

    for r in range(tm):
        for k in range(TOP_K):
            row_copy(r, k).start()

    def wait_body(r, c):
        for k in range(TOP_K):
            row_copy(r, k).wait()
        return c
    lax.fori_loop(0, tm, wait_body, 0, unroll=4)


def _dispatch(pos, pad_block_row, n_used, xn, n_blocks, tm, bm):
    t = xn.shape[0]
    grid_spec = pltpu.PrefetchScalarGridSpec(
        num_scalar_prefetch=3,
        grid=(t // tm,),
        in_specs=[pl.BlockSpec((tm, D_MODEL), lambda i, ps, pr, nu: (i, 0))],
        out_specs=pl.BlockSpec(memory_space=pl.ANY),
        scratch_shapes=[pltpu.VMEM((bm, D_MODEL), F32), pltpu.SemaphoreType.DMA(()),
                        pltpu.SemaphoreType.DMA(())],
    )
    return pl.pallas_call(
        functools.partial(_dispatch_kernel, tm=tm, bm=bm, n_blocks=n_blocks),
        grid_spec=grid_spec,
        out_shape=jax.ShapeDtypeStruct((n_blocks * bm, D_MODEL), F32),
        compiler_params=_cparams(("arbitrary",)),
        name="moe_dispatch",
    )(pos, pad_block_row, n_used, xn)


def _moe_kernel(nused_ref, be_ref, xs_ref, wgu_ref, bgu_ref, wd_ref, bd_ref, ys_ref):
    i = pl.program_id(0)

    @pl.when(i < nused_ref[0])
    def _():
        x = xs_ref[...].astype(BF16)
        gu = jnp.dot(x, wgu_ref[0], preferred_element_type=F32) + bgu_ref[0]
        gate = jnp.minimum(gu[:, :D_FF], SWIGLU_LIMIT)
        up = jnp.clip(gu[:, D_FF:], -SWIGLU_LIMIT, SWIGLU_LIMIT)
        act = gate * jax.nn.sigmoid(SWIGLU_ALPHA * gate) * (up + 1.0)
        ys_ref[...] = jnp.dot(act.astype(BF16), wd_ref[0], preferred_element_type=F32) + bd_ref[0]

    @pl.when(i >= nused_ref[0])
    def _():
        ys_ref[...] = jnp.zeros_like(ys_ref)


def _moe(n_used, block_e, xs, wgu, bgu, wd, bd, bm):
    n_blocks = block_e.shape[0]
    grid_spec = pltpu.PrefetchScalarGridSpec(
        num_scalar_prefetch=2,
        grid=(n_blocks,),
        in_specs=[
            pl.BlockSpec((bm, D_MODEL), lambda i, nu, be: (i, 0)),
            pl.BlockSpec((1, D_MODEL, 2 * D_FF), lambda i, nu, be: (be[i], 0, 0)),
            pl.BlockSpec((1, 1, 2 * D_FF), lambda i, nu, be: (be[i], 0, 0)),
            pl.BlockSpec((1, D_FF, D_MODEL), lambda i, nu, be: (be[i], 0, 0)),
            pl.BlockSpec((1, 1, D_MODEL), lambda i, nu, be: (be[i], 0, 0)),
        ],
        out_specs=pl.BlockSpec((bm, D_MODEL), lambda i, nu, be: (i, 0)),
    )
    return pl.pallas_call(
        _moe_kernel,
        grid_spec=grid_spec,
        out_shape=jax.ShapeDtypeStruct((n_blocks * bm, D_MODEL), F32),
        compiler_params=_cparams(("arbitrary",)),
        name="moe_experts",
    )(n_used, block_e, xs, wgu, bgu, wd, bd)


def _combine_kernel(pos_ref, x1_ref, gate_ref, ys_hbm, o_ref, buf, sem, *, tm):
    i = pl.program_id(0)
    n_steps = pl.num_programs(0)
    slot = i % 2

    def row_copy(step, s, r, k):
        tok = step * tm + r
        return pltpu.make_async_copy(ys_hbm.at[pl.ds(pos_ref[tok * TOP_K + k], 1)],
                                     buf.at[s, k, pl.ds(r, 1)], sem.at[s])

    def start_rows(step, s):
        for r in range(tm):
            for k in range(TOP_K):
                row_copy(step, s, r, k).start()

    @pl.when(i == 0)
    def _():
        start_rows(0, 0)

    for s in range(2):
        @pl.when(jnp.logical_and(i + 1 < n_steps, slot == 1 - s))
        def _():
            start_rows(i + 1, s)

    def wait_body(r, c):
        for k in range(TOP_K):
            row_copy(i, slot, r, k).wait()
        return c
    lax.fori_loop(0, tm, wait_body, 0, unroll=4)
    g = gate_ref[...]
    acc = x1_ref[...]
    for k in range(TOP_K):
        acc = acc + g[:, k:k + 1] * buf[slot, k]
    o_ref[...] = acc


def _combine(pos, x1, gates, ys, tm):
    t = x1.shape[0]
    grid_spec = pltpu.PrefetchScalarGridSpec(
        num_scalar_prefetch=1,
        grid=(t // tm,),
        in_specs=[pl.BlockSpec((tm, D_MODEL), lambda i, ps: (i, 0)),
                  pl.BlockSpec((tm, LANE), lambda i, ps: (i, 0)),
                  pl.BlockSpec(memory_space=pl.ANY)],
        out_specs=pl.BlockSpec((tm, D_MODEL), lambda i, ps: (i, 0)),
        scratch_shapes=[pltpu.VMEM((2, TOP_K, tm, D_MODEL), F32), pltpu.SemaphoreType.DMA((2,))],
    )
    return pl.pallas_call(
        functools.partial(_combine_kernel, tm=tm),
        grid_spec=grid_spec,
        out_shape=jax.ShapeDtypeStruct((t, D_MODEL), F32),
        compiler_params=_cparams(("arbitrary",)),
        name="moe_combine",
    )(pos, x1, gates, ys)


def _rope_tables_t(first_pos, count):
    inv = ROPE_THETA ** (-jnp.arange(0, ROT_DIMS, 2, dtype=F32) / ROT_DIMS)
    ang = (first_pos + jnp.arange(count, dtype=F32))[:, None] * inv[None, :]
    return jnp.cos(ang).T, jnp.sin(ang).T


def _pick(n, candidates):
    for c in candidates:
        if n % c == 0:
            return c
    raise ValueError(f"no tile for {n}")


def kernel(x_prompt, x_sample, meta_tokens, norm1_g, w_in, diff_q_g, diff_k_g, lam_q1, lam_k1, lam_q2, lam_k2, diff_sub_g, w_diff_out, na_q_g, na_k_g, na_rpb, na_meta_bias, w_na_out, w_o, norm2_g, w_router, b_router, w_gate_up, b_gate_up, w_down, b_down):
    l = 0
    col = lambda v: v.astype(F32).reshape(-1, 1)
    rowv = lambda v: v.astype(F32).reshape(1, -1)
    g1 = rowv(norm1_g[l])
    w_in_b = w_in[l].astype(BF16)
    w_in_t = w_in_b[:, :N_FM_SEG * D_MODEL].T
    w_in_g = w_in_b[:, N_FM_SEG * D_MODEL:]
    gqd, gkd, gqn, gkn = col(diff_q_g[l]), col(diff_k_g[l]), col(na_q_g[l]), col(na_k_g[l])
    lq1, lk1, lq2, lk2 = rowv(lam_q1[l]), rowv(lam_k1[l]), rowv(lam_q2[l]), rowv(lam_k2[l])
    subg = col(diff_sub_g[l])
    wd_b, wn_b, wo_b = w_diff_out[l].astype(BF16), w_na_out[l].astype(BF16), w_o[l].astype(BF16)
    g2 = rowv(norm2_g[l])
    wr_hi = w_router[l].astype(BF16)
    wr_lo = (w_router[l].astype(F32) - wr_hi.astype(F32)).astype(BF16)
    wr = jnp.concatenate([wr_hi, wr_lo], axis=1)
    br = rowv(b_router[l])
    diff_bounded = _scores_bounded(diff_q_g[l], diff_k_g[l])
    na_bias_max = LOG2E * jnp.maximum(jnp.max(jnp.abs(na_rpb[l])), jnp.max(jnp.abs(na_meta_bias[l])))
    na_bounded = _scores_bounded(na_q_g[l], na_k_g[l], na_bias_max)
    wgu_b = w_gate_up[l].astype(BF16)
    bgu = b_gate_up[l].astype(F32).reshape(N_EXPERTS, 1, 2 * D_FF)
    wdn_b = w_down[l].astype(BF16)
    bdn = b_down[l].astype(F32).reshape(N_EXPERTS, 1, D_MODEL)

    meta_pad = jnp.zeros((LANE, D_MODEL), F32).at[:N_META].set(meta_tokens.astype(F32))
    cos_m, sin_m = _rope_tables_t(0.0, LANE)
    _, kd_m, vd_t_m, _, kn_m, vn_t_m, _, _ = _in_proj(meta_pad, g1, w_in_t, w_in_g, cos_m, sin_m,
                                                      gqd, gkd, gqn, gkn, LANE)
    kd_m, kn_m = kd_m[:N_META], kn_m[:N_META]
    vd_t_m, vn_t_m = vd_t_m[:, :N_META], vn_t_m[:, :N_META]

    def trunk(x):
        batch, n, _ = x.shape
        t = batch * n
        x2d = x.reshape(t, D_MODEL)
        tm = _pick(n, (512, 256, 128))
        cos_t, sin_t = _rope_tables_t(float(N_META), n)
        qd_t, kd, vd_t, qn_t, kn, vn_t, ga, gb = _in_proj(x2d, g1, w_in_t, w_in_g, cos_t, sin_t,
                                                         gqd, gkd, gqn, gkn, tm)
        tq = _pick(n, (1024, 512, 256, 128))
        o_a = _diff_attn(diff_bounded, qd_t, kd, vd_t, kd_m, vd_t_m, lq1, lk1, lq2, lk2, subg,
                         batch, n, tq, _pick(n, (512, 256, 128)))
        plan, sigs = _na_plan(n // GRID_W)
        if tuple(sigs) not in bias_cache:
            bias_cache[tuple(sigs)] = _na_bias_tables(na_rpb[l], na_meta_bias[l], sigs)
        bias, mbias = bias_cache[tuple(sigs)]
        o_b = _na_attn(na_bounded, qn_t, kn, vn_t, kn_m, vn_t_m, bias, mbias, plan, batch, n)
        x1, xn, topi, gates = _out_proj(o_a, o_b, ga, gb, x2d, wd_b, wn_b, wo_b, g2, wr, br, tm)
        bm = 256
        pos, block_e, n_used, pad_block_row = _route_plan(topi[:, :TOP_K], t, bm)
        xs = _dispatch(pos, pad_block_row, n_used, xn, block_e.shape[0], 256, bm)
        ys = _moe(n_used, block_e, xs, wgu_b, bgu, wdn_b, bdn, bm)
        y = _combine(pos, x1, gates, ys, LANE)
        return y.reshape(batch, n, D_MODEL)

    bias_cache = {}

    return (trunk(x_prompt), trunk(x_sample))
```

```python
import functools
import math

import numpy as np
import jax
import jax.numpy as jnp
from jax import lax
from jax.experimental import pallas as pl
from jax.experimental.pallas import tpu as pltpu

D_MODEL = 1024
N_META = 16
GRID_W = 64
DIFF_HEADS = 8
DIFF_DH = 64
DIFF_VD = 2 * DIFF_DH
NA_HEADS = 16
NA_DH = 64
NA_WIN_H = 8
NA_WIN_W = 16
ROT_DIMS = DIFF_DH // 4
ROPE_THETA = 500000.0
N_EXPERTS = 32
TOP_K = 4
D_FF = D_MODEL
SWIGLU_ALPHA = 1.702
SWIGLU_LIMIT = 7.0
EPS = 1e-6
NEG = -1e30
LAM_INIT = 0.8 - 0.6 * math.exp(-0.3 * 0)
LOG2E = 1.4426950408889634
Q_SCALE = (DIFF_DH ** -0.5) * LOG2E
SCORE_BOUND = 60.0

N_SEG = 8
LANE = 128
NA_ROWS_PER_STEP = 2
NA_KEY_ROWS = NA_WIN_H + NA_ROWS_PER_STEP
VMEM_LIMIT = 56 * 1024 * 1024

F32 = jnp.float32
BF16 = jnp.bfloat16


def _cparams(sem):
    return pltpu.CompilerParams(dimension_semantics=sem, vmem_limit_bytes=VMEM_LIMIT)


def _head_norm_t(y_t, g_ref, cos, sin, scale):
    tm = y_t.shape[1]
    y3 = y_t.reshape(D_MODEL // DIFF_DH, DIFF_DH, tm)
    ms = jnp.mean(y3 * y3, axis=1, keepdims=True)
    y3 = y3 * lax.rsqrt(ms + EPS) * g_ref[...].reshape(1, DIFF_DH, 1)
    if cos is not None:
        half = ROT_DIMS // 2
        x1 = y3[:, 0:half, :]
        x2 = y3[:, half:ROT_DIMS, :]
        c = cos[None]
        s = sin[None]
        y3 = jnp.concatenate([x1 * c - x2 * s, x2 * c + x1 * s, y3[:, ROT_DIMS:, :]], axis=1)
    if scale != 1.0:
        y3 = y3 * scale
    return y3.reshape(D_MODEL, tm)


def _in_proj_kernel(x_ref, g1_ref, wt_ref, wg_ref, cos_ref, sin_ref, gqd_ref, gkd_ref, gqn_ref, gkn_ref,
                    qd_t_ref, kd_ref, vd_t_ref, qn_t_ref, kn_ref, vn_t_ref, ga_ref, gb_ref,
                    h_ref, ht_ref):
    j = pl.program_id(1)

    @pl.when(j == 0)
    def _():
        x = x_ref[...]
        ms = jnp.mean(x * x, axis=-1, keepdims=True)
        h = x * lax.rsqrt(ms + EPS) * g1_ref[...]
        h_ref[...] = h.astype(BF16)
        ht_ref[...] = h.T.astype(BF16)

    def proj_t():
        return jnp.dot(wt_ref[...], ht_ref[...], preferred_element_type=F32)

    def proj():
        return jnp.dot(h_ref[...], wg_ref[...], preferred_element_type=F32)

    @pl.when(j == 0)
    def _():
        qd_t_ref[...] = _head_norm_t(proj_t(), gqd_ref, cos_ref[...], sin_ref[...], Q_SCALE).astype(BF16)

    @pl.when(j == 1)
    def _():
        kd_ref[...] = _head_norm_t(proj_t(), gkd_ref, cos_ref[...], sin_ref[...], 1.0).T.astype(BF16)

    @pl.when(j == 2)
    def _():
        vd_t_ref[...] = proj_t().astype(BF16)

    @pl.when(j == 3)
    def _():
        qn_t_ref[...] = _head_norm_t(proj_t(), gqn_ref, None, None, Q_SCALE).astype(BF16)

    @pl.when(j == 4)
    def _():
        kn_ref[...] = _head_norm_t(proj_t(), gkn_ref, None, None, 1.0).T.astype(BF16)

    @pl.when(j == 5)
    def _():
        vn_t_ref[...] = proj_t().astype(BF16)

    @pl.when(j == 6)
    def _():
        ga_ref[...] = jax.nn.sigmoid(proj()).astype(BF16)

    @pl.when(j == 7)
    def _():
        gb_ref[...] = jax.nn.sigmoid(proj()).astype(BF16)


N_FM_SEG = 6


def _in_proj(x2d, g1, w_t, w_g, cos_t, sin_t, gqd, gkd, gqn, gkn, tm):
    t = x2d.shape[0]
    n_pos_blocks = cos_t.shape[1] // tm
    tok_major = pl.BlockSpec((tm, D_MODEL), lambda i, j: (i, 0))
    feat_major = pl.BlockSpec((D_MODEL, tm), lambda i, j: (0, i))
    small = lambda shape: pl.BlockSpec(shape, lambda i, j: (0, 0))
    tm_shape = jax.ShapeDtypeStruct((t, D_MODEL), BF16)
    fm_shape = jax.ShapeDtypeStruct((D_MODEL, t), BF16)
    return pl.pallas_call(
        _in_proj_kernel,
        grid=(t // tm, N_SEG),
        in_specs=[
            tok_major,
            small((1, D_MODEL)),
            pl.BlockSpec((D_MODEL, D_MODEL), lambda i, j: (jnp.minimum(j, N_FM_SEG - 1), 0)),
            pl.BlockSpec((D_MODEL, D_MODEL), lambda i, j: (0, jnp.maximum(j - N_FM_SEG, 0))),
            pl.BlockSpec((ROT_DIMS // 2, tm), lambda i, j: (0, i % n_pos_blocks)),
            pl.BlockSpec((ROT_DIMS // 2, tm), lambda i, j: (0, i % n_pos_blocks)),
            small((DIFF_DH, 1)), small((DIFF_DH, 1)), small((NA_DH, 1)), small((NA_DH, 1)),
        ],
        out_specs=[feat_major, tok_major, feat_major, feat_major, tok_major, feat_major,
                   tok_major, tok_major],
        out_shape=[fm_shape, tm_shape, fm_shape, fm_shape, tm_shape, fm_shape, tm_shape, tm_shape],
        scratch_shapes=[pltpu.VMEM((tm, D_MODEL), BF16), pltpu.VMEM((D_MODEL, tm), BF16)],
        compiler_params=_cparams(("parallel", "arbitrary")),
        name="in_proj",
    )(x2d, g1, w_t, w_g, cos_t, sin_t, gqd, gkd, gqn, gkn)


def _split_halves(q_t):
    row = lax.broadcasted_iota(jnp.int32, q_t.shape, 0)
    zero = jnp.zeros_like(q_t)
    return jnp.concatenate([jnp.where(row < DIFF_DH, q_t, zero),
                            jnp.where(row >= DIFF_DH, q_t, zero)], axis=1)


def _diff_attn_kernel(bounded_ref, q_t_ref, k_ref, v_t_ref, km_ref, vm_t_ref, lq1_ref, lk1_ref, lq2_ref,
                      lk2_ref, subg_ref, o_ref, acc_ref, l_ref, *, tk):
    tq = q_t_ref.shape[1]
    n = k_ref.shape[0]
    qz = _split_halves(q_t_ref[...])

    @pl.when(bounded_ref[0] != 0)
    def _():
        p = jnp.exp2(jnp.dot(km_ref[...], qz, preferred_element_type=F32))
        l = jnp.sum(p, axis=0, keepdims=True)
        acc = jnp.dot(vm_t_ref[...], p.astype(BF16), preferred_element_type=F32)
        for c in range(n // tk):
            p = jnp.exp2(jnp.dot(k_ref[c * tk:(c + 1) * tk, :], qz, preferred_element_type=F32))
            l = l + jnp.sum(p, axis=0, keepdims=True)
            acc = acc + jnp.dot(v_t_ref[:, c * tk:(c + 1) * tk], p.astype(BF16),
                                preferred_element_type=F32)
        acc_ref[...] = acc
        l_ref[...] = l

    @pl.when(bounded_ref[0] == 0)
    def _():
        s = jnp.dot(km_ref[...], qz, preferred_element_type=F32)
        m = jnp.max(s, axis=0, keepdims=True)
        p = jnp.exp2(s - m)
        l = jnp.sum(p, axis=0, keepdims=True)
        acc = jnp.dot(vm_t_ref[...], p.astype(BF16), preferred_element_type=F32)
        for c in range(n // tk):
            s = jnp.dot(k_ref[c * tk:(c + 1) * tk, :], qz, preferred_element_type=F32)
            m_new = jnp.maximum(m, jnp.max(s, axis=0, keepdims=True))
            alpha = jnp.exp2(m - m_new)
            p = jnp.exp2(s - m_new)
            l = alpha * l + jnp.sum(p, axis=0, keepdims=True)
            acc = alpha * acc + jnp.dot(v_t_ref[:, c * tk:(c + 1) * tk], p.astype(BF16),
                                        preferred_element_type=F32)
            m = m_new
        acc_ref[...] = acc
        l_ref[...] = l

    lam = (jnp.exp(jnp.sum(lq1_ref[...] * lk1_ref[...], axis=-1, keepdims=True))
           - jnp.exp(jnp.sum(lq2_ref[...] * lk2_ref[...], axis=-1, keepdims=True)) + LAM_INIT)
    o = acc_ref[...] / l_ref[...]
    o_t = o[:, :tq] - lam * o[:, tq:]
    ms = jnp.mean(o_t * o_t, axis=0, keepdims=True)
    o_t = o_t * lax.rsqrt(ms + EPS) * subg_ref[...] * (1.0 - LAM_INIT)
    o_ref[...] = o_t.T.astype(BF16)


def _scores_bounded(gq, gk, extra=0.0):
    bound = DIFF_DH * Q_SCALE * jnp.max(jnp.abs(gq)) * jnp.max(jnp.abs(gk)) + extra
    return (bound <= SCORE_BOUND).astype(jnp.int32).reshape(1)


def _diff_attn(bounded, qd_t, kd, vd_t, kd_m, vd_t_m, lq1, lk1, lq2, lk2, subg, batch, n, tq, tk):
    nq = n // tq
    row = lambda shape: pl.BlockSpec(shape, lambda b, h, qi, bd: (0, 0))
    grid_spec = pltpu.PrefetchScalarGridSpec(
        num_scalar_prefetch=1,
        grid=(batch, DIFF_HEADS, nq),
        in_specs=[
            pl.BlockSpec((DIFF_VD, tq), lambda b, h, qi, bd: (h, b * nq + qi)),
            pl.BlockSpec((n, DIFF_VD), lambda b, h, qi, bd: (b, h)),
            pl.BlockSpec((DIFF_VD, n), lambda b, h, qi, bd: (h, b)),
            pl.BlockSpec((N_META, DIFF_VD), lambda b, h, qi, bd: (0, h)),
            pl.BlockSpec((DIFF_VD, N_META), lambda b, h, qi, bd: (h, 0)),
            row((1, DIFF_DH)), row((1, DIFF_DH)), row((1, DIFF_DH)), row((1, DIFF_DH)),
            row((DIFF_VD, 1)),
        ],
        out_specs=pl.BlockSpec((tq, DIFF_VD), lambda b, h, qi, bd: (b * nq + qi, h)),
        scratch_shapes=[pltpu.VMEM((DIFF_VD, 2 * tq), F32), pltpu.VMEM((1, 2 * tq), F32)],
    )
    return pl.pallas_call(
        functools.partial(_diff_attn_kernel, tk=tk),
        grid_spec=grid_spec,
        out_shape=jax.ShapeDtypeStruct((batch * n, D_MODEL), BF16),
        compiler_params=_cparams(("parallel", "parallel", "arbitrary")),
        name="diff_attn",
    )(bounded, qd_t, kd, vd_t, kd_m, vd_t_m, lq1, lk1, lq2, lk2, subg)


def _na_plan(rows):
    wh = min(NA_WIN_H, rows)
    assert wh == NA_WIN_H and rows % NA_ROWS_PER_STEP == 0 and rows >= NA_KEY_ROWS
    start = lambda r: int(np.clip(r - wh // 2, 0, rows - wh))
    sigs, plan = [], []
    for r0 in range(0, rows, NA_ROWS_PER_STEP):
        kr0 = int(np.clip(r0 - wh // 2, 0, rows - NA_KEY_ROWS))
        assert kr0 % 2 == 0
        sig = (r0 - kr0,) + tuple(start(r0 + g) - kr0 for g in range(NA_ROWS_PER_STEP))
        if sig not in sigs:
            sigs.append(sig)
        plan.append((kr0, sigs.index(sig)))
    return plan, sigs


def _na_bias_tables(rpb, meta_bias, sigs):
    nk = NA_KEY_ROWS * GRID_W
    nqr = NA_ROWS_PER_STEP * GRID_W
    kc = np.arange(GRID_W)[:, None]
    qc = np.arange(GRID_W)[None, :]
    cs = np.clip(qc - NA_WIN_W // 2, 0, GRID_W - NA_WIN_W)
    col_ok = (kc >= cs) & (kc < cs + NA_WIN_W)
    dc = np.clip(kc - qc, -(NA_WIN_W - 1), NA_WIN_W - 1) + NA_WIN_W - 1
    toe = jnp.where(col_ok, rpb.astype(F32)[:, :, dc] * LOG2E, NEG)
    neg = jnp.full((NA_HEADS, GRID_W, GRID_W), NEG, F32)
    per_class = []
    for sig in sigs:
        dq, starts = sig[0], sig[1:]
        key_rows = []
        for jr in range(NA_KEY_ROWS):
            blocks = []
            for g in range(NA_ROWS_PER_STEP):
                in_window = starts[g] <= jr < starts[g] + NA_WIN_H
                dr = jr - dq - g + NA_WIN_H - 1
                assert not in_window or 0 <= dr < 2 * NA_WIN_H - 1
                blocks.append(toe[:, dr] if in_window else neg)
            key_rows.append(jnp.concatenate(blocks, axis=-1))
        per_class.append(jnp.concatenate(key_rows, axis=-2))
    bias = jnp.stack(per_class, axis=1)
    c = len(sigs)
    bias = bias.reshape(NA_HEADS // 2, 2, c, nk, nqr).transpose(0, 2, 3, 1, 4)
    bias = bias.reshape(NA_HEADS // 2, c, nk, 2 * nqr)
    mb = (meta_bias.astype(F32) * LOG2E).reshape(NA_HEADS // 2, 2, N_META)
    mb = jnp.broadcast_to(mb.transpose(0, 2, 1)[:, :, :, None], (NA_HEADS // 2, N_META, 2, nqr))
    return bias, mb.reshape(NA_HEADS // 2, N_META, 2 * nqr)


def _na_kernel(bounded_ref, q_t_ref, k_ref, v_t_ref, km_ref, vm_t_ref, bias_ref, mbias_ref, o_ref, *, plan):
    nqr = NA_ROWS_PER_STEP * GRID_W
    nk = NA_KEY_ROWS * GRID_W

    def all_rows(shifted):
        km = km_ref[...]
        vm_t = vm_t_ref[...]
        mbias = mbias_ref[0]
        for rp, (kr0, cls) in enumerate(plan):
            q0 = rp * nqr
            k0 = kr0 * GRID_W
            qz = _split_halves(q_t_ref[:, q0:q0 + nqr])
            s = jnp.dot(k_ref[k0:k0 + nk, :], qz, preferred_element_type=F32) + bias_ref[0, cls]
            sm = jnp.dot(km, qz, preferred_element_type=F32) + mbias
            if shifted:
                m = jnp.maximum(jnp.max(s, axis=0, keepdims=True), jnp.max(sm, axis=0, keepdims=True))
                s = s - m
                sm = sm - m
            p = jnp.exp2(s)
            pm = jnp.exp2(sm)
            l = jnp.sum(p, axis=0, keepdims=True) + jnp.sum(pm, axis=0, keepdims=True)
            acc = (jnp.dot(v_t_ref[:, k0:k0 + nk], p.astype(BF16), preferred_element_type=F32)
                   + jnp.dot(vm_t, pm.astype(BF16), preferred_element_type=F32))
            o = acc / l
            o_t = jnp.concatenate([o[:NA_DH, :nqr], o[NA_DH:, nqr:]], axis=0)
            o_ref[q0:q0 + nqr, :] = o_t.T.astype(BF16)

    @pl.when(bounded_ref[0] != 0)
    def _():
        all_rows(shifted=False)

    @pl.when(bounded_ref[0] == 0)
    def _():
        all_rows(shifted=True)


def _na_attn(bounded, qn_t, kn, vn_t, kn_m, vn_t_m, bias, mbias, plan, batch, n):
    n_cls = bias.shape[1]
    hp = NA_HEADS // 2
    grid_spec = pltpu.PrefetchScalarGridSpec(
        num_scalar_prefetch=1,
        grid=(batch, hp),
        in_specs=[
            pl.BlockSpec((2 * NA_DH, n), lambda b, h, bd: (h, b)),
            pl.BlockSpec((n, 2 * NA_DH), lambda b, h, bd: (b, h)),
            pl.BlockSpec((2 * NA_DH, n), lambda b, h, bd: (h, b)),
            pl.BlockSpec((N_META, 2 * NA_DH), lambda b, h, bd: (0, h)),
            pl.BlockSpec((2 * NA_DH, N_META), lambda b, h, bd: (h, 0)),
            pl.BlockSpec((1, n_cls) + bias.shape[2:], lambda b, h, bd: (h, 0, 0, 0)),
            pl.BlockSpec((1,) + mbias.shape[1:], lambda b, h, bd: (h, 0, 0)),
        ],
        out_specs=pl.BlockSpec((n, 2 * NA_DH), lambda b, h, bd: (b, h)),
    )
    return pl.pallas_call(
        functools.partial(_na_kernel, plan=plan),
        grid_spec=grid_spec,
        out_shape=jax.ShapeDtypeStruct((batch * n, D_MODEL), BF16),
        compiler_params=_cparams(("parallel", "arbitrary")),
        name="na_attn",
    )(bounded, qn_t, kn, vn_t, kn_m, vn_t_m, bias, mbias)


def _out_proj_kernel(oa_ref, ob_ref, ga_ref, gb_ref, x_ref, wd_ref, wn_ref, wo_ref, g2_ref, wr_ref,
                     br_ref, x1_ref, xn_ref, topi_ref, gate_ref):
    ya = jnp.dot(oa_ref[...], wd_ref[...], preferred_element_type=F32)
    yb = jnp.dot(ob_ref[...], wn_ref[...], preferred_element_type=F32)
    merged = ga_ref[...].astype(F32) * ya + gb_ref[...].astype(F32) * yb
    x1 = x_ref[...] + jnp.dot(merged.astype(BF16), wo_ref[...], preferred_element_type=F32)
    x1_ref[...] = x1
    ms = jnp.mean(x1 * x1, axis=-1, keepdims=True)
    xn = x1 * lax.rsqrt(ms + EPS) * g2_ref[...]
    xn_ref[...] = xn
    x_hi = xn.astype(BF16)
    x_lo = (xn - x_hi.astype(F32)).astype(BF16)
    hh = jnp.dot(x_hi, wr_ref[...], preferred_element_type=F32)
    lh = jnp.dot(x_lo, wr_ref[:, :N_EXPERTS], preferred_element_type=F32)
    logits = hh[:, :N_EXPERTS] + (hh[:, N_EXPERTS:] + lh) + br_ref[...]
    tm = logits.shape[0]
    lane_e = lax.broadcasted_iota(jnp.int32, logits.shape, 1)
    lane_o = lax.broadcasted_iota(jnp.int32, (tm, LANE), 1)
    topi = jnp.zeros((tm, LANE), jnp.int32)
    topv = jnp.zeros((tm, LANE), F32)
    v0 = None
    den = jnp.zeros((tm, 1), F32)
    for k in range(TOP_K):
        vk = jnp.max(logits, axis=-1, keepdims=True)
        ik = jnp.min(jnp.where(logits == vk, lane_e, N_EXPERTS), axis=-1, keepdims=True)
        logits = jnp.where(lane_e == ik, -jnp.inf, logits)
        if k == 0:
            v0 = vk
        ek = jnp.exp(vk - v0)
        den = den + ek
        topi = jnp.where(lane_o == k, ik, topi)
        topv = jnp.where(lane_o == k, ek, topv)
    topi_ref[...] = topi
    gate_ref[...] = topv / den


def _out_proj(o_a, o_b, ga, gb, x2d, wd, wn, wo, g2, wr, br, tm):
    t = x2d.shape[0]
    tok = pl.BlockSpec((tm, D_MODEL), lambda i: (i, 0))
    full = lambda shape: pl.BlockSpec(shape, lambda i: (0, 0))
    narrow = pl.BlockSpec((tm, LANE), lambda i: (i, 0))
    return pl.pallas_call(
        _out_proj_kernel,
        grid=(t // tm,),
        in_specs=[tok, tok, tok, tok, tok,
                  full((D_MODEL, D_MODEL)), full((D_MODEL, D_MODEL)), full((D_MODEL, D_MODEL)),
                  full((1, D_MODEL)), full((D_MODEL, 2 * N_EXPERTS)), full((1, N_EXPERTS))],
        out_specs=[tok, tok, narrow, narrow],
        out_shape=[jax.ShapeDtypeStruct((t, D_MODEL), F32), jax.ShapeDtypeStruct((t, D_MODEL), F32),
                   jax.ShapeDtypeStruct((t, LANE), jnp.int32), jax.ShapeDtypeStruct((t, LANE), F32)],
        compiler_params=_cparams(("parallel",)),
        name="out_proj",
    )(o_a, o_b, ga, gb, x2d, wd, wn, wo, g2, wr, br)


def _route_plan(topi, t, bm):
    e_ids = jnp.arange(N_EXPERTS, dtype=jnp.int32)
    onehot = topi[:, :, None] == e_ids[None, None, :]
    routed = jnp.sum(onehot, axis=1).astype(jnp.int32)
    incl = jnp.cumsum(routed, axis=0)
    counts = incl[-1]
    padded = (counts + bm - 1) // bm * bm
    pend = jnp.cumsum(padded)
    pstart = pend - padded
    slot_of = (pstart[None, :] + incl - routed)[:, None, :]
    pos = jnp.sum(jnp.where(onehot, slot_of, 0), axis=-1).astype(jnp.int32)
    n_blocks = -(-(t * TOP_K + N_EXPERTS * (bm - 1)) // bm)
    block_first = jnp.arange(n_blocks, dtype=jnp.int32) * bm
    block_e = jnp.minimum(jnp.sum((block_first[:, None] >= pend[None, :]).astype(jnp.int32), axis=1),
                          N_EXPERTS - 1)
    n_used = (pend[-1:] // bm).astype(jnp.int32)
    pad_block_row = jnp.where(padded > 0, pend - bm, -1).astype(jnp.int32)
    return pos.reshape(-1), block_e, n_used, pad_block_row


def _dispatch_kernel(pos_ref, padrow_ref, nused_ref, xn_ref, xs_hbm, zbuf, zsem, sem, *, tm, bm, n_blocks):
    i = pl.program_id(0)

    def zero_copy(row0):
        return pltpu.make_async_copy(zbuf, xs_hbm.at[pl.ds(pl.multiple_of(row0, bm), bm)], zsem)

    def for_each_zero_block(fn):
        for e in range(N_EXPERTS):
            @pl.when(padrow_ref[e] >= 0)
            def _():
                fn(zero_copy(padrow_ref[e]))

        def tail(b, c):
            fn(zero_copy(b * bm))
            return c
        lax.fori_loop(nused_ref[0], n_blocks, tail, 0)

    @pl.when(i == 0)
    def _():
        zbuf[...] = jnp.zeros_like(zbuf)
        for_each_zero_block(lambda cp: cp.start())
        for_each_zero_block(lambda cp: cp.wait())

    def row_copy(r, k):
        return pltpu.make_async_copy(xn_ref.at[pl.ds(r, 1)],
                                     xs_hbm.at[pl.ds(pos_ref[(i * tm + r) * TOP_K + k], 1)], sem)

    for r in range(tm):
        for k in range(TOP_K):
            row_copy(r, k).start(priority=k % 2)

    def wait_body(r, c):
        for k in range(TOP_K):
            row_copy(r, k).wait()
        return c
    lax.fori_loop(0, tm, wait_body, 0, unroll=4)


def _dispatch(pos, pad_block_row, n_used, xn, n_blocks, tm, bm):
    t = xn.shape[0]
    grid_spec = pltpu.PrefetchScalarGridSpec(
        num_scalar_prefetch=3,
        grid=(t // tm,),
        in_specs=[pl.BlockSpec((tm, D_MODEL), lambda i, ps, pr, nu: (i, 0))],
        out_specs=pl.BlockSpec(memory_space=pl.ANY),
        scratch_shapes=[pltpu.VMEM((bm, D_MODEL), F32), pltpu.SemaphoreType.DMA(()),
                        pltpu.SemaphoreType.DMA(())],
    )
    return pl.pallas_call(
        functools.partial(_dispatch_kernel, tm=tm, bm=bm, n_blocks=n_blocks),
        grid_spec=grid_spec,
        out_shape=jax.ShapeDtypeStruct((n_blocks * bm, D_MODEL), F32),
        compiler_params=_cparams(("arbitrary",)),
        name="moe_dispatch",
    )(pos, pad_block_row, n_used, xn)


def _moe_kernel(nused_ref, be_ref, xs_ref, wgu_ref, bgu_ref, wd_ref, bd_ref, ys_ref, wgu_b, wd_b):
    i = pl.program_id(0)

    @pl.when(i < nused_ref[0])
    def _():
        new_expert = jnp.logical_or(i == 0, be_ref[i] != be_ref[jnp.maximum(i - 1, 0)])

        @pl.when(new_expert)
        def _():
            wgu_b[...] = wgu_ref[0].astype(BF16)
            wd_b[...] = wd_ref[0].astype(BF16)

        x = xs_ref[...].astype(BF16)
        gu = jnp.dot(x, wgu_b[...], preferred_element_type=F32) + bgu_ref[0]
        gate = jnp.minimum(gu[:, :D_FF], SWIGLU_LIMIT)
        up = jnp.clip(gu[:, D_FF:], -SWIGLU_LIMIT, SWIGLU_LIMIT)
        act = gate * jax.nn.sigmoid(SWIGLU_ALPHA * gate) * (up + 1.0)
        ys_ref[...] = jnp.dot(act.astype(BF16), wd_b[...], preferred_element_type=F32) + bd_ref[0]

    @pl.when(i >= nused_ref[0])
    def _():
        ys_ref[...] = jnp.zeros_like(ys_ref)


def _moe(n_used, block_e, xs, wgu, bgu, wd, bd, bm):
    n_blocks = block_e.shape[0]
    grid_spec = pltpu.PrefetchScalarGridSpec(
        num_scalar_prefetch=2,
        grid=(n_blocks,),
        in_specs=[
            pl.BlockSpec((bm, D_MODEL), lambda i, nu, be: (i, 0)),
            pl.BlockSpec((1, D_MODEL, 2 * D_FF), lambda i, nu, be: (be[i], 0, 0)),
            pl.BlockSpec((1, 1, 2 * D_FF), lambda i, nu, be: (be[i], 0, 0)),
            pl.BlockSpec((1, D_FF, D_MODEL), lambda i, nu, be: (be[i], 0, 0)),
            pl.BlockSpec((1, 1, D_MODEL), lambda i, nu, be: (be[i], 0, 0)),
        ],
        out_specs=pl.BlockSpec((bm, D_MODEL), lambda i, nu, be: (i, 0)),
        scratch_shapes=[pltpu.VMEM((D_MODEL, 2 * D_FF), BF16), pltpu.VMEM((D_FF, D_MODEL), BF16)],
    )
    return pl.pallas_call(
        _moe_kernel,
        grid_spec=grid_spec,
        out_shape=jax.ShapeDtypeStruct((n_blocks * bm, D_MODEL), F32),
        compiler_params=_cparams(("arbitrary",)),
        name="moe_experts",
    )(n_used, block_e, xs, wgu, bgu, wd, bd)


def _combine_kernel(pos_ref, x1_ref, gate_ref, ys_hbm, o_ref, buf, sem, *, tm):
    i = pl.program_id(0)
    n_steps = pl.num_programs(0)
    slot = i % 2

    def row_copy(step, s, r, k):
        tok = step * tm + r
        return pltpu.make_async_copy(ys_hbm.at[pl.ds(pos_ref[tok * TOP_K + k], 1)],
                                     buf.at[s, k, pl.ds(r, 1)], sem.at[s])

    def start_rows(step, s):
        for r in range(tm):
            for k in range(TOP_K):
                row_copy(step, s, r, k).start(priority=k % 2)

    @pl.when(i == 0)
    def _():
        start_rows(0, 0)

    for s in range(2):
        @pl.when(jnp.logical_and(i + 1 < n_steps, slot == 1 - s))
        def _():
            start_rows(i + 1, s)

    def wait_body(r, c):
        for k in range(TOP_K):
            row_copy(i, slot, r, k).wait()
        return c
    lax.fori_loop(0, tm, wait_body, 0, unroll=4)
    g = gate_ref[...]
    acc = x1_ref[...]
    for k in range(TOP_K):
        acc = acc + g[:, k:k + 1] * buf[slot, k]
    o_ref[...] = acc


def _combine(pos, x1, gates, ys, tm):
    t = x1.shape[0]
    grid_spec = pltpu.PrefetchScalarGridSpec(
        num_scalar_prefetch=1,
        grid=(t // tm,),
        in_specs=[pl.BlockSpec((tm, D_MODEL), lambda i, ps: (i, 0)),
                  pl.BlockSpec((tm, LANE), lambda i, ps: (i, 0)),
                  pl.BlockSpec(memory_space=pl.ANY)],
        out_specs=pl.BlockSpec((tm, D_MODEL), lambda i, ps: (i, 0)),
        scratch_shapes=[pltpu.VMEM((2, TOP_K, tm, D_MODEL), F32), pltpu.SemaphoreType.DMA((2,))],
    )
    return pl.pallas_call(
        functools.partial(_combine_kernel, tm=tm),
        grid_spec=grid_spec,
        out_shape=jax.ShapeDtypeStruct((t, D_MODEL), F32),
        compiler_params=_cparams(("arbitrary",)),
        name="moe_combine",
    )(pos, x1, gates, ys)


def _rope_tables_t(first_pos, count):
    inv = ROPE_THETA ** (-jnp.arange(0, ROT_DIMS, 2, dtype=F32) / ROT_DIMS)
    ang = (first_pos + jnp.arange(count, dtype=F32))[:, None] * inv[None, :]
    return jnp.cos(ang).T, jnp.sin(ang).T


def _pick(n, candidates):
    for c in candidates:
        if n % c == 0:
            return c
    raise ValueError(f"no tile for {n}")


def kernel(x_prompt, x_sample, meta_tokens, norm1_g, w_in, diff_q_g, diff_k_g, lam_q1, lam_k1, lam_q2, lam_k2, diff_sub_g, w_diff_out, na_q_g, na_k_g, na_rpb, na_meta_bias, w_na_out, w_o, norm2_g, w_router, b_router, w_gate_up, b_gate_up, w_down, b_down):
    l = 0
    col = lambda v: v.astype(F32).reshape(-1, 1)
    rowv = lambda v: v.astype(F32).reshape(1, -1)
    g1 = rowv(norm1_g[l])
    w_in_b = w_in[l].astype(BF16)
    w_in_t = w_in_b[:, :N_FM_SEG * D_MODEL].T
    w_in_g = w_in_b[:, N_FM_SEG * D_MODEL:]
    gqd, gkd, gqn, gkn = col(diff_q_g[l]), col(diff_k_g[l]), col(na_q_g[l]), col(na_k_g[l])
    lq1, lk1, lq2, lk2 = rowv(lam_q1[l]), rowv(lam_k1[l]), rowv(lam_q2[l]), rowv(lam_k2[l])
    subg = col(diff_sub_g[l])
    wd_b, wn_b, wo_b = w_diff_out[l].astype(BF16), w_na_out[l].astype(BF16), w_o[l].astype(BF16)
    g2 = rowv(norm2_g[l])
    wr_hi = w_router[l].astype(BF16)
    wr_lo = (w_router[l].astype(F32) - wr_hi.astype(F32)).astype(BF16)
    wr = jnp.concatenate([wr_hi, wr_lo], axis=1)
    br = rowv(b_router[l])
    diff_bounded = _scores_bounded(diff_q_g[l], diff_k_g[l])
    na_bias_max = LOG2E * jnp.maximum(jnp.max(jnp.abs(na_rpb[l])), jnp.max(jnp.abs(na_meta_bias[l])))
    na_bounded = _scores_bounded(na_q_g[l], na_k_g[l], na_bias_max)
    wgu_b = w_gate_up[l]
    bgu = b_gate_up[l].astype(F32).reshape(N_EXPERTS, 1, 2 * D_FF)
    wdn_b = w_down[l]
    bdn = b_down[l].astype(F32).reshape(N_EXPERTS, 1, D_MODEL)

    meta_pad = jnp.zeros((LANE, D_MODEL), F32).at[:N_META].set(meta_tokens.astype(F32))
    cos_m, sin_m = _rope_tables_t(0.0, LANE)
    _, kd_m, vd_t_m, _, kn_m, vn_t_m, _, _ = _in_proj(meta_pad, g1, w_in_t, w_in_g, cos_m, sin_m,
                                                      gqd, gkd, gqn, gkn, LANE)
    kd_m, kn_m = kd_m[:N_META], kn_m[:N_META]
    vd_t_m, vn_t_m = vd_t_m[:, :N_META], vn_t_m[:, :N_META]

    def trunk(x):
        batch, n, _ = x.shape
        t = batch * n
        x2d = x.reshape(t, D_MODEL)
        tm = _pick(n, (512, 256, 128))
        cos_t, sin_t = _rope_tables_t(float(N_META), n)
        qd_t, kd, vd_t, qn_t, kn, vn_t, ga, gb = _in_proj(x2d, g1, w_in_t, w_in_g, cos_t, sin_t,
                                                         gqd, gkd, gqn, gkn, tm)
        tq = _pick(n, (1024, 512, 256, 128))
        o_a = _diff_attn(diff_bounded, qd_t, kd, vd_t, kd_m, vd_t_m, lq1, lk1, lq2, lk2, subg,
                         batch, n, tq, _pick(n, (512, 256, 128)))
        plan, sigs = _na_plan(n // GRID_W)
        if tuple(sigs) not in bias_cache:
            bias_cache[tuple(sigs)] = _na_bias_tables(na_rpb[l], na_meta_bias[l], sigs)
        bias, mbias = bias_cache[tuple(sigs)]
        o_b = _na_attn(na_bounded, qn_t, kn, vn_t, kn_m, vn_t_m, bias, mbias, plan, batch, n)
        x1, xn, topi, gates = _out_proj(o_a, o_b, ga, gb, x2d, wd_b, wn_b, wo_b, g2, wr, br, tm)
        bm = 256
        pos, block_e, n_used, pad_block_row = _route_plan(topi[:, :TOP_K], t, bm)
        xs = _dispatch(pos, pad_block_row, n_used, xn, block_e.shape[0], 256, bm)
        ys = _moe(n_used, block_e, xs, wgu_b, bgu, wdn_b, bdn, bm)
        y = _combine(pos, x1, gates, ys, LANE)
        return y.reshape(batch, n, D_MODEL)

    bias_cache = {}

    return (trunk(x_prompt), trunk(x_sample))
```

```python
import functools
import math

import numpy as np
import jax
import jax.numpy as jnp
from jax import lax
from jax.experimental import pallas as pl
from jax.experimental.pallas import tpu as pltpu

D_MODEL = 1024
N_META = 16
GRID_W = 64
DIFF_HEADS = 8
DIFF_DH = 64
DIFF_VD = 2 * DIFF_DH
NA_HEADS = 16
NA_DH = 64
NA_WIN_H = 8
NA_WIN_W = 16
ROT_DIMS = DIFF_DH // 4
ROPE_THETA = 500000.0
N_EXPERTS = 32
TOP_K = 4
D_FF = D_MODEL
SWIGLU_ALPHA = 1.702
SWIGLU_LIMIT = 7.0
EPS = 1e-6
NEG = -1e30
LAM_INIT = 0.8 - 0.6 * math.exp(-0.3 * 0)
LOG2E = 1.4426950408889634
Q_SCALE = (DIFF_DH ** -0.5) * LOG2E
SCORE_BOUND = 60.0

N_SEG = 8
LANE = 128
NA_ROWS_PER_STEP = 2
NA_KEY_ROWS = NA_WIN_H + NA_ROWS_PER_STEP
VMEM_LIMIT = 56 * 1024 * 1024

F32 = jnp.float32
BF16 = jnp.bfloat16


def _cparams(sem):
    return pltpu.CompilerParams(dimension_semantics=sem, vmem_limit_bytes=VMEM_LIMIT)


def _head_norm_t(y_t, g_ref, cos, sin, scale):
    tm = y_t.shape[1]
    y3 = y_t.reshape(D_MODEL // DIFF_DH, DIFF_DH, tm)
    ms = jnp.mean(y3 * y3, axis=1, keepdims=True)
    y3 = y3 * lax.rsqrt(ms + EPS) * g_ref[...].reshape(1, DIFF_DH, 1)
    if cos is not None:
        half = ROT_DIMS // 2
        x1 = y3[:, 0:half, :]
        x2 = y3[:, half:ROT_DIMS, :]
        c = cos[None]
        s = sin[None]
        y3 = jnp.concatenate([x1 * c - x2 * s, x2 * c + x1 * s, y3[:, ROT_DIMS:, :]], axis=1)
    if scale != 1.0:
        y3 = y3 * scale
    return y3.reshape(D_MODEL, tm)


def _in_proj_kernel(x_ref, g1_ref, wt_ref, wg_ref, cos_ref, sin_ref, gqd_ref, gkd_ref, gqn_ref, gkn_ref,
                    qd_t_ref, kd_ref, vd_t_ref, qn_t_ref, kn_ref, vn_t_ref, ga_ref, gb_ref,
                    h_ref, ht_ref):
    j = pl.program_id(1)

    @pl.when(j == 0)
    def _():
        x = x_ref[...]
        ms = jnp.mean(x * x, axis=-1, keepdims=True)
        h = x * lax.rsqrt(ms + EPS) * g1_ref[...]
        h_ref[...] = h.astype(BF16)
        ht_ref[...] = h.T.astype(BF16)

    def proj_t(half):
        return jnp.dot(wt_ref[half * D_MODEL:(half + 1) * D_MODEL, :], ht_ref[...],
                       preferred_element_type=F32)

    def proj(half):
        return jnp.dot(h_ref[...], wg_ref[:, half * D_MODEL:(half + 1) * D_MODEL],
                       preferred_element_type=F32)

    @pl.when(j == 0)
    def _():
        qd_t_ref[...] = _head_norm_t(proj_t(0), gqd_ref, cos_ref[...], sin_ref[...], Q_SCALE).astype(BF16)
        kd_ref[...] = _head_norm_t(proj_t(1), gkd_ref, cos_ref[...], sin_ref[...], 1.0).T.astype(BF16)

    @pl.when(j == 1)
    def _():
        vd_t_ref[...] = proj_t(0).astype(BF16)
        qn_t_ref[...] = _head_norm_t(proj_t(1), gqn_ref, None, None, Q_SCALE).astype(BF16)

    @pl.when(j == 2)
    def _():
        kn_ref[...] = _head_norm_t(proj_t(0), gkn_ref, None, None, 1.0).T.astype(BF16)
        vn_t_ref[...] = proj_t(1).astype(BF16)

    @pl.when(j == 3)
    def _():
        ga_ref[...] = jax.nn.sigmoid(proj(0)).astype(BF16)
        gb_ref[...] = jax.nn.sigmoid(proj(1)).astype(BF16)


N_FM_SEG = 6
SEG_PER_STEP = 2


def _in_proj(x2d, g1, w_t, w_g, cos_t, sin_t, gqd, gkd, gqn, gkn, tm):
    t = x2d.shape[0]
    n_pos_blocks = cos_t.shape[1] // tm
    tok_major = pl.BlockSpec((tm, D_MODEL), lambda i, j: (i, 0))
    feat_major = pl.BlockSpec((D_MODEL, tm), lambda i, j: (0, i))
    small = lambda shape: pl.BlockSpec(shape, lambda i, j: (0, 0))
    tm_shape = jax.ShapeDtypeStruct((t, D_MODEL), BF16)
    fm_shape = jax.ShapeDtypeStruct((D_MODEL, t), BF16)
    n_fm_steps = N_FM_SEG // SEG_PER_STEP
    return pl.pallas_call(
        _in_proj_kernel,
        grid=(t // tm, N_SEG // SEG_PER_STEP),
        in_specs=[
            tok_major,
            small((1, D_MODEL)),
            pl.BlockSpec((SEG_PER_STEP * D_MODEL, D_MODEL), lambda i, j: (jnp.minimum(j, n_fm_steps - 1), 0)),
            small((D_MODEL, SEG_PER_STEP * D_MODEL)),
            pl.BlockSpec((ROT_DIMS // 2, tm), lambda i, j: (0, i % n_pos_blocks)),
            pl.BlockSpec((ROT_DIMS // 2, tm), lambda i, j: (0, i % n_pos_blocks)),
            small((DIFF_DH, 1)), small((DIFF_DH, 1)), small((NA_DH, 1)), small((NA_DH, 1)),
        ],
        out_specs=[feat_major, tok_major, feat_major, feat_major, tok_major, feat_major,
                   tok_major, tok_major],
        out_shape=[fm_shape, tm_shape, fm_shape, fm_shape, tm_shape, fm_shape, tm_shape, tm_shape],
        scratch_shapes=[pltpu.VMEM((tm, D_MODEL), BF16), pltpu.VMEM((D_MODEL, tm), BF16)],
        compiler_params=_cparams(("parallel", "arbitrary")),
        name="in_proj",
    )(x2d, g1, w_t, w_g, cos_t, sin_t, gqd, gkd, gqn, gkn)


def _split_halves(q_t):
    row = lax.broadcasted_iota(jnp.int32, q_t.shape, 0)
    zero = jnp.zeros_like(q_t)
    return jnp.concatenate([jnp.where(row < DIFF_DH, q_t, zero),
                            jnp.where(row >= DIFF_DH, q_t, zero)], axis=1)


def _diff_attn_kernel(bounded_ref, q_t_ref, k_ref, v_t_ref, km_ref, vm_t_ref, lq1_ref, lk1_ref, lq2_ref,
                      lk2_ref, subg_ref, o_ref, acc_ref, l_ref, *, tk):
    tq = q_t_ref.shape[1]
    n = k_ref.shape[0]
    qz = _split_halves(q_t_ref[...])

    @pl.when(bounded_ref[0] != 0)
    def _():
        p = jnp.exp2(jnp.dot(km_ref[...], qz, preferred_element_type=F32))
        l = jnp.sum(p, axis=0, keepdims=True)
        acc = jnp.dot(vm_t_ref[...], p.astype(BF16), preferred_element_type=F32)
        for c in range(n // tk):
            p = jnp.exp2(jnp.dot(k_ref[c * tk:(c + 1) * tk, :], qz, preferred_element_type=F32))
            l = l + jnp.sum(p, axis=0, keepdims=True)
            acc = acc + jnp.dot(v_t_ref[:, c * tk:(c + 1) * tk], p.astype(BF16),
                                preferred_element_type=F32)
        acc_ref[...] = acc
        l_ref[...] = l

    @pl.when(bounded_ref[0] == 0)
    def _():
        s = jnp.dot(km_ref[...], qz, preferred_element_type=F32)
        m = jnp.max(s, axis=0, keepdims=True)
        p = jnp.exp2(s - m)
        l = jnp.sum(p, axis=0, keepdims=True)
        acc = jnp.dot(vm_t_ref[...], p.astype(BF16), preferred_element_type=F32)
        for c in range(n // tk):
            s = jnp.dot(k_ref[c * tk:(c + 1) * tk, :], qz, preferred_element_type=F32)
            m_new = jnp.maximum(m, jnp.max(s, axis=0, keepdims=True))
            alpha = jnp.exp2(m - m_new)
            p = jnp.exp2(s - m_new)
            l = alpha * l + jnp.sum(p, axis=0, keepdims=True)
            acc = alpha * acc + jnp.dot(v_t_ref[:, c * tk:(c + 1) * tk], p.astype(BF16),
                                        preferred_element_type=F32)
            m = m_new
        acc_ref[...] = acc
        l_ref[...] = l

    lam = (jnp.exp(jnp.sum(lq1_ref[...] * lk1_ref[...], axis=-1, keepdims=True))
           - jnp.exp(jnp.sum(lq2_ref[...] * lk2_ref[...], axis=-1, keepdims=True)) + LAM_INIT)
    o = acc_ref[...] / l_ref[...]
    o_t = o[:, :tq] - lam * o[:, tq:]
    ms = jnp.mean(o_t * o_t, axis=0, keepdims=True)
    o_t = o_t * lax.rsqrt(ms + EPS) * subg_ref[...] * (1.0 - LAM_INIT)
    o_ref[...] = o_t.T.astype(BF16)


def _scores_bounded(gq, gk, extra=0.0):
    bound = DIFF_DH * Q_SCALE * jnp.max(jnp.abs(gq)) * jnp.max(jnp.abs(gk)) + extra
    return (bound <= SCORE_BOUND).astype(jnp.int32).reshape(1)


def _diff_attn(bounded, qd_t, kd, vd_t, kd_m, vd_t_m, lq1, lk1, lq2, lk2, subg, batch, n, tq, tk):
    nq = n // tq
    row = lambda shape: pl.BlockSpec(shape, lambda b, h, qi, bd: (0, 0))
    grid_spec = pltpu.PrefetchScalarGridSpec(
        num_scalar_prefetch=1,
        grid=(batch, DIFF_HEADS, nq),
        in_specs=[
            pl.BlockSpec((DIFF_VD, tq), lambda b, h, qi, bd: (h, b * nq + qi)),
            pl.BlockSpec((n, DIFF_VD), lambda b, h, qi, bd: (b, h)),
            pl.BlockSpec((DIFF_VD, n), lambda b, h, qi, bd: (h, b)),
            pl.BlockSpec((N_META, DIFF_VD), lambda b, h, qi, bd: (0, h)),
            pl.BlockSpec((DIFF_VD, N_META), lambda b, h, qi, bd: (h, 0)),
            row((1, DIFF_DH)), row((1, DIFF_DH)), row((1, DIFF_DH)), row((1, DIFF_DH)),
            row((DIFF_VD, 1)),
        ],
        out_specs=pl.BlockSpec((tq, DIFF_VD), lambda b, h, qi, bd: (b * nq + qi, h)),
        scratch_shapes=[pltpu.VMEM((DIFF_VD, 2 * tq), F32), pltpu.VMEM((1, 2 * tq), F32)],
    )
    return pl.pallas_call(
        functools.partial(_diff_attn_kernel, tk=tk),
        grid_spec=grid_spec,
        out_shape=jax.ShapeDtypeStruct((batch * n, D_MODEL), BF16),
        compiler_params=_cparams(("parallel", "parallel", "arbitrary")),
        name="diff_attn",
    )(bounded, qd_t, kd, vd_t, kd_m, vd_t_m, lq1, lk1, lq2, lk2, subg)


def _na_plan(rows):
    wh = min(NA_WIN_H, rows)
    assert wh == NA_WIN_H and rows % NA_ROWS_PER_STEP == 0 and rows >= NA_KEY_ROWS
    start = lambda r: int(np.clip(r - wh // 2, 0, rows - wh))
    sigs, plan = [], []
    for r0 in range(0, rows, NA_ROWS_PER_STEP):
        kr0 = int(np.clip(r0 - wh // 2, 0, rows - NA_KEY_ROWS))
        assert kr0 % 2 == 0
        sig = (r0 - kr0,) + tuple(start(r0 + g) - kr0 for g in range(NA_ROWS_PER_STEP))
        if sig not in sigs:
            sigs.append(sig)
        plan.append((kr0, sigs.index(sig)))
    return plan, sigs


def _na_bias_tables(rpb, meta_bias, sigs):
    nk = NA_KEY_ROWS * GRID_W
    nqr = NA_ROWS_PER_STEP * GRID_W
    kc = np.arange(GRID_W)[:, None]
    qc = np.arange(GRID_W)[None, :]
    cs = np.clip(qc - NA_WIN_W // 2, 0, GRID_W - NA_WIN_W)
    col_ok = (kc >= cs) & (kc < cs + NA_WIN_W)
    dc = np.clip(kc - qc, -(NA_WIN_W - 1), NA_WIN_W - 1) + NA_WIN_W - 1
    toe = jnp.where(col_ok, rpb.astype(F32)[:, :, dc] * LOG2E, NEG)
    neg = jnp.full((NA_HEADS, GRID_W, GRID_W), NEG, F32)
    per_class = []
    for sig in sigs:
        dq, starts = sig[0], sig[1:]
        key_rows = []
        for jr in range(NA_KEY_ROWS):
            blocks = []
            for g in range(NA_ROWS_PER_STEP):
                in_window = starts[g] <= jr < starts[g] + NA_WIN_H
                dr = jr - dq - g + NA_WIN_H - 1
                assert not in_window or 0 <= dr < 2 * NA_WIN_H - 1
                blocks.append(toe[:, dr] if in_window else neg)
            key_rows.append(jnp.concatenate(blocks, axis=-1))
        per_class.append(jnp.concatenate(key_rows, axis=-2))
    bias = jnp.stack(per_class, axis=1)
    c = len(sigs)
    bias = bias.reshape(NA_HEADS // 2, 2, c, nk, nqr).transpose(0, 2, 3, 1, 4)
    bias = bias.reshape(NA_HEADS // 2, c, nk, 2 * nqr)
    mb = (meta_bias.astype(F32) * LOG2E).reshape(NA_HEADS // 2, 2, N_META)
    mb = jnp.broadcast_to(mb.transpose(0, 2, 1)[:, :, :, None], (NA_HEADS // 2, N_META, 2, nqr))
    return bias, mb.reshape(NA_HEADS // 2, N_META, 2 * nqr)


def _na_kernel(bounded_ref, q_t_ref, k_ref, v_t_ref, km_ref, vm_t_ref, bias_ref, mbias_ref, o_ref, *, plan):
    nqr = NA_ROWS_PER_STEP * GRID_W
    nk = NA_KEY_ROWS * GRID_W

    def all_rows(shifted):
        km = km_ref[...]
        vm_t = vm_t_ref[...]
        mbias = mbias_ref[0]
        for rp, (kr0, cls) in enumerate(plan):
            q0 = rp * nqr
            k0 = kr0 * GRID_W
            qz = _split_halves(q_t_ref[:, q0:q0 + nqr])
            s = jnp.dot(k_ref[k0:k0 + nk, :], qz, preferred_element_type=F32) + bias_ref[0, cls]
            sm = jnp.dot(km, qz, preferred_element_type=F32) + mbias
            if shifted:
                m = jnp.maximum(jnp.max(s, axis=0, keepdims=True), jnp.max(sm, axis=0, keepdims=True))
                s = s - m
                sm = sm - m
            p = jnp.exp2(s)
            pm = jnp.exp2(sm)
            l = jnp.sum(p, axis=0, keepdims=True) + jnp.sum(pm, axis=0, keepdims=True)
            acc = (jnp.dot(v_t_ref[:, k0:k0 + nk], p.astype(BF16), preferred_element_type=F32)
                   + jnp.dot(vm_t, pm.astype(BF16), preferred_element_type=F32))
            o = acc / l
            o_t = jnp.concatenate([o[:NA_DH, :nqr], o[NA_DH:, nqr:]], axis=0)
            o_ref[q0:q0 + nqr, :] = o_t.T.astype(BF16)

    @pl.when(bounded_ref[0] != 0)
    def _():
        all_rows(shifted=False)

    @pl.when(bounded_ref[0] == 0)
    def _():
        all_rows(shifted=True)


def _na_attn(bounded, qn_t, kn, vn_t, kn_m, vn_t_m, bias, mbias, plan, batch, n):
    n_cls = bias.shape[1]
    hp = NA_HEADS // 2
    grid_spec = pltpu.PrefetchScalarGridSpec(
        num_scalar_prefetch=1,
        grid=(batch, hp),
        in_specs=[
            pl.BlockSpec((2 * NA_DH, n), lambda b, h, bd: (h, b)),
            pl.BlockSpec((n, 2 * NA_DH), lambda b, h, bd: (b, h)),
            pl.BlockSpec((2 * NA_DH, n), lambda b, h, bd: (h, b)),
            pl.BlockSpec((N_META, 2 * NA_DH), lambda b, h, bd: (0, h)),
            pl.BlockSpec((2 * NA_DH, N_META), lambda b, h, bd: (h, 0)),
            pl.BlockSpec((1, n_cls) + bias.shape[2:], lambda b, h, bd: (h, 0, 0, 0)),
            pl.BlockSpec((1,) + mbias.shape[1:], lambda b, h, bd: (h, 0, 0)),
        ],
        out_specs=pl.BlockSpec((n, 2 * NA_DH), lambda b, h, bd: (b, h)),
    )
    return pl.pallas_call(
        functools.partial(_na_kernel, plan=plan),
        grid_spec=grid_spec,
        out_shape=jax.ShapeDtypeStruct((batch * n, D_MODEL), BF16),
        compiler_params=_cparams(("parallel", "arbitrary")),
        name="na_attn",
    )(bounded, qn_t, kn, vn_t, kn_m, vn_t_m, bias, mbias)


def _out_proj_kernel(oa_ref, ob_ref, ga_ref, gb_ref, x_ref, wd_ref, wn_ref, wo_ref, g2_ref, wr_ref,
                     br_ref, x1_ref, xn_ref, topi_ref, gate_ref):
    ya = jnp.dot(oa_ref[...], wd_ref[...], preferred_element_type=F32)
    yb = jnp.dot(ob_ref[...], wn_ref[...], preferred_element_type=F32)
    merged = ga_ref[...].astype(F32) * ya + gb_ref[...].astype(F32) * yb
    x1 = x_ref[...] + jnp.dot(merged.astype(BF16), wo_ref[...], preferred_element_type=F32)
    x1_ref[...] = x1
    ms = jnp.mean(x1 * x1, axis=-1, keepdims=True)
    xn = x1 * lax.rsqrt(ms + EPS) * g2_ref[...]
    xn_ref[...] = xn
    x_hi = xn.astype(BF16)
    x_lo = (xn - x_hi.astype(F32)).astype(BF16)
    hh = jnp.dot(x_hi, wr_ref[...], preferred_element_type=F32)
    lh = jnp.dot(x_lo, wr_ref[:, :N_EXPERTS], preferred_element_type=F32)
    logits = hh[:, :N_EXPERTS] + (hh[:, N_EXPERTS:] + lh) + br_ref[...]
    tm = logits.shape[0]
    lane_e = lax.broadcasted_iota(jnp.int32, logits.shape, 1)
    lane_o = lax.broadcasted_iota(jnp.int32, (tm, LANE), 1)
    topi = jnp.zeros((tm, LANE), jnp.int32)
    topv = jnp.zeros((tm, LANE), F32)
    v0 = None
    den = jnp.zeros((tm, 1), F32)
    for k in range(TOP_K):
        vk = jnp.max(logits, axis=-1, keepdims=True)
        ik = jnp.min(jnp.where(logits == vk, lane_e, N_EXPERTS), axis=-1, keepdims=True)
        logits = jnp.where(lane_e == ik, -jnp.inf, logits)
        if k == 0:
            v0 = vk
        ek = jnp.exp(vk - v0)
        den = den + ek
        topi = jnp.where(lane_o == k, ik, topi)
        topv = jnp.where(lane_o == k, ek, topv)
    topi_ref[...] = topi
    gate_ref[...] = topv / den


def _out_proj(o_a, o_b, ga, gb, x2d, wd, wn, wo, g2, wr, br, tm):
    t = x2d.shape[0]
    tok = pl.BlockSpec((tm, D_MODEL), lambda i: (i, 0))
    full = lambda shape: pl.BlockSpec(shape, lambda i: (0, 0))
    narrow = pl.BlockSpec((tm, LANE), lambda i: (i, 0))
    return pl.pallas_call(
        _out_proj_kernel,
        grid=(t // tm,),
        in_specs=[tok, tok, tok, tok, tok,
                  full((D_MODEL, D_MODEL)), full((D_MODEL, D_MODEL)), full((D_MODEL, D_MODEL)),
                  full((1, D_MODEL)), full((D_MODEL, 2 * N_EXPERTS)), full((1, N_EXPERTS))],
        out_specs=[tok, tok, narrow, narrow],
        out_shape=[jax.ShapeDtypeStruct((t, D_MODEL), F32), jax.ShapeDtypeStruct((t, D_MODEL), F32),
                   jax.ShapeDtypeStruct((t, LANE), jnp.int32), jax.ShapeDtypeStruct((t, LANE), F32)],
        compiler_params=_cparams(("parallel",)),
        name="out_proj",
    )(o_a, o_b, ga, gb, x2d, wd, wn, wo, g2, wr, br)


def _route_plan(topi, t, bm):
    e_ids = jnp.arange(N_EXPERTS, dtype=jnp.int32)
    onehot = topi[:, :, None] == e_ids[None, None, :]
    routed = jnp.sum(onehot, axis=1).astype(jnp.int32)
    incl = jnp.cumsum(routed, axis=0)
    counts = incl[-1]
    padded = (counts + bm - 1) // bm * bm
    pend = jnp.cumsum(padded)
    pstart = pend - padded
    slot_of = (pstart[None, :] + incl - routed)[:, None, :]
    pos = jnp.sum(jnp.where(onehot, slot_of, 0), axis=-1).astype(jnp.int32)
    n_blocks = -(-(t * TOP_K + N_EXPERTS * (bm - 1)) // bm)
    block_first = jnp.arange(n_blocks, dtype=jnp.int32) * bm
    block_e = jnp.minimum(jnp.sum((block_first[:, None] >= pend[None, :]).astype(jnp.int32), axis=1),
                          N_EXPERTS - 1)
    n_used = (pend[-1:] // bm).astype(jnp.int32)
    pad_block_row = jnp.where(padded > 0, pend - bm, -1).astype(jnp.int32)
    return pos.reshape(-1), block_e, n_used, pad_block_row


def _dispatch_kernel(pos_ref, padrow_ref, nused_ref, xn_ref, xs_hbm, zbuf, zsem, sem, *, tm, bm, n_blocks):
    i = pl.program_id(0)

    def zero_copy(row0):
        return pltpu.make_async_copy(zbuf, xs_hbm.at[pl.ds(pl.multiple_of(row0, bm), bm)], zsem)

    def for_each_zero_block(fn):
        for e in range(N_EXPERTS):
            @pl.when(padrow_ref[e] >= 0)
            def _():
                fn(zero_copy(padrow_ref[e]))

        def tail(b, c):
            fn(zero_copy(b * bm))
            return c
        lax.fori_loop(nused_ref[0], n_blocks, tail, 0)

    @pl.when(i == 0)
    def _():
        zbuf[...] = jnp.zeros_like(zbuf)
        for_each_zero_block(lambda cp: cp.start())
        for_each_zero_block(lambda cp: cp.wait())

    def row_copy(r, k):
        return pltpu.make_async_copy(xn_ref.at[pl.ds(r, 1)],
                                     xs_hbm.at[pl.ds(pos_ref[(i * tm + r) * TOP_K + k], 1)], sem)

    for r in range(tm):
        for k in range(TOP_K):
            row_copy(r, k).start(priority=k % 2)

    def wait_body(r, c):
        for k in range(TOP_K):
            row_copy(r, k).wait()
        return c
    lax.fori_loop(0, tm, wait_body, 0, unroll=4)


def _dispatch(pos, pad_block_row, n_used, xn, n_blocks, tm, bm):
    t = xn.shape[0]
    grid_spec = pltpu.PrefetchScalarGridSpec(
        num_scalar_prefetch=3,
        grid=(t // tm,),
        in_specs=[pl.BlockSpec((tm, D_MODEL), lambda i, ps, pr, nu: (i, 0))],
        out_specs=pl.BlockSpec(memory_space=pl.ANY),
        scratch_shapes=[pltpu.VMEM((bm, D_MODEL), F32), pltpu.SemaphoreType.DMA(()),
                        pltpu.SemaphoreType.DMA(())],
    )
    return pl.pallas_call(
        functools.partial(_dispatch_kernel, tm=tm, bm=bm, n_blocks=n_blocks),
        grid_spec=grid_spec,
        out_shape=jax.ShapeDtypeStruct((n_blocks * bm, D_MODEL), F32),
        compiler_params=_cparams(("arbitrary",)),
        name="moe_dispatch",
    )(pos, pad_block_row, n_used, xn)


def _moe_kernel(nused_ref, be_ref, xs_ref, wgu_ref, bgu_ref, wd_ref, bd_ref, ys_ref, wgu_b, wd_b):
    i = pl.program_id(0)

    @pl.when(i < nused_ref[0])
    def _():
        new_expert = jnp.logical_or(i == 0, be_ref[i] != be_ref[jnp.maximum(i - 1, 0)])

        @pl.when(new_expert)
        def _():
            wgu_b[...] = wgu_ref[0].astype(BF16)
            wd_b[...] = wd_ref[0].astype(BF16)

        x = xs_ref[...].astype(BF16)
        gu = jnp.dot(x, wgu_b[...], preferred_element_type=F32) + bgu_ref[0]
        gate = jnp.minimum(gu[:, :D_FF], SWIGLU_LIMIT)
        up = jnp.clip(gu[:, D_FF:], -SWIGLU_LIMIT, SWIGLU_LIMIT)
        act = gate * jax.nn.sigmoid(SWIGLU_ALPHA * gate) * (up + 1.0)
        ys_ref[...] = jnp.dot(act.astype(BF16), wd_b[...], preferred_element_type=F32) + bd_ref[0]

    @pl.when(i >= nused_ref[0])
    def _():
        ys_ref[...] = jnp.zeros_like(ys_ref)


def _moe(n_used, block_e, xs, wgu, bgu, wd, bd, bm):
    n_blocks = block_e.shape[0]
    grid_spec = pltpu.PrefetchScalarGridSpec(
        num_scalar_prefetch=2,
        grid=(n_blocks,),
        in_specs=[
            pl.BlockSpec((bm, D_MODEL), lambda i, nu, be: (i, 0)),
            pl.BlockSpec((1, D_MODEL, 2 * D_FF), lambda i, nu, be: (be[i], 0, 0)),
            pl.BlockSpec((1, 1, 2 * D_FF), lambda i, nu, be: (be[i], 0, 0)),
            pl.BlockSpec((1, D_FF, D_MODEL), lambda i, nu, be: (be[i], 0, 0)),
            pl.BlockSpec((1, 1, D_MODEL), lambda i, nu, be: (be[i], 0, 0)),
        ],
        out_specs=pl.BlockSpec((bm, D_MODEL), lambda i, nu, be: (i, 0)),
        scratch_shapes=[pltpu.VMEM((D_MODEL, 2 * D_FF), BF16), pltpu.VMEM((D_FF, D_MODEL), BF16)],
    )
    return pl.pallas_call(
        _moe_kernel,
        grid_spec=grid_spec,
        out_shape=jax.ShapeDtypeStruct((n_blocks * bm, D_MODEL), F32),
        compiler_params=_cparams(("arbitrary",)),
        name="moe_experts",
    )(n_used, block_e, xs, wgu, bgu, wd, bd)


def _combine_kernel(pos_ref, x1_ref, gate_ref, ys_hbm, o_ref, buf, sem, *, tm):
    i = pl.program_id(0)
    n_steps = pl.num_programs(0)
    slot = i % 2

    def row_copy(step, s, r, k):
        tok = step * tm + r
        return pltpu.make_async_copy(ys_hbm.at[pl.ds(pos_ref[tok * TOP_K + k], 1)],
                                     buf.at[s, k, pl.ds(r, 1)], sem.at[s])

    def start_rows(step, s):
        for r in range(tm):
            for k in range(TOP_K):
                row_copy(step, s, r, k).start(priority=k % 2)

    @pl.when(i == 0)
    def _():
        start_rows(0, 0)

    for s in range(2):
        @pl.when(jnp.logical_and(i + 1 < n_steps, slot == 1 - s))
        def _():
            start_rows(i + 1, s)

    def wait_body(r, c):
        for k in range(TOP_K):
            row_copy(i, slot, r, k).wait()
        return c
    lax.fori_loop(0, tm, wait_body, 0, unroll=4)
    g = gate_ref[...]
    acc = x1_ref[...]
    for k in range(TOP_K):
        acc = acc + g[:, k:k + 1] * buf[slot, k]
    o_ref[...] = acc


def _combine(pos, x1, gates, ys, tm):
    t = x1.shape[0]
    grid_spec = pltpu.PrefetchScalarGridSpec(
        num_scalar_prefetch=1,
        grid=(t // tm,),
        in_specs=[pl.BlockSpec((tm, D_MODEL), lambda i, ps: (i, 0)),
                  pl.BlockSpec((tm, LANE), lambda i, ps: (i, 0)),
                  pl.BlockSpec(memory_space=pl.ANY)],
        out_specs=pl.BlockSpec((tm, D_MODEL), lambda i, ps: (i, 0)),
        scratch_shapes=[pltpu.VMEM((2, TOP_K, tm, D_MODEL), F32), pltpu.SemaphoreType.DMA((2,))],
    )
    return pl.pallas_call(
        functools.partial(_combine_kernel, tm=tm),
        grid_spec=grid_spec,
        out_shape=jax.ShapeDtypeStruct((t, D_MODEL), F32),
        compiler_params=_cparams(("arbitrary",)),
        name="moe_combine",
    )(pos, x1, gates, ys)


def _rope_tables_t(first_pos, count):
    inv = ROPE_THETA ** (-jnp.arange(0, ROT_DIMS, 2, dtype=F32) / ROT_DIMS)
    ang = (first_pos + jnp.arange(count, dtype=F32))[:, None] * inv[None, :]
    return jnp.cos(ang).T, jnp.sin(ang).T


def _pick(n, candidates):
    for c in candidates:
        if n % c == 0:
            return c
    raise ValueError(f"no tile for {n}")


def kernel(x_prompt, x_sample, meta_tokens, norm1_g, w_in, diff_q_g, diff_k_g, lam_q1, lam_k1, lam_q2, lam_k2, diff_sub_g, w_diff_out, na_q_g, na_k_g, na_rpb, na_meta_bias, w_na_out, w_o, norm2_g, w_router, b_router, w_gate_up, b_gate_up, w_down, b_down):
    l = 0
    col = lambda v: v.astype(F32).reshape(-1, 1)
    rowv = lambda v: v.astype(F32).reshape(1, -1)
    g1 = rowv(norm1_g[l])
    w_in_b = w_in[l].astype(BF16)
    w_in_t = w_in_b[:, :N_FM_SEG * D_MODEL].T
    w_in_g = w_in_b[:, N_FM_SEG * D_MODEL:]
    gqd, gkd, gqn, gkn = col(diff_q_g[l]), col(diff_k_g[l]), col(na_q_g[l]), col(na_k_g[l])
    lq1, lk1, lq2, lk2 = rowv(lam_q1[l]), rowv(lam_k1[l]), rowv(lam_q2[l]), rowv(lam_k2[l])
    subg = col(diff_sub_g[l])
    wd_b, wn_b, wo_b = w_diff_out[l].astype(BF16), w_na_out[l].astype(BF16), w_o[l].astype(BF16)
    g2 = rowv(norm2_g[l])
    wr_hi = w_router[l].astype(BF16)
    wr_lo = (w_router[l].astype(F32) - wr_hi.astype(F32)).astype(BF16)
    wr = jnp.concatenate([wr_hi, wr_lo], axis=1)
    br = rowv(b_router[l])
    diff_bounded = _scores_bounded(diff_q_g[l], diff_k_g[l])
    na_bias_max = LOG2E * jnp.maximum(jnp.max(jnp.abs(na_rpb[l])), jnp.max(jnp.abs(na_meta_bias[l])))
    na_bounded = _scores_bounded(na_q_g[l], na_k_g[l], na_bias_max)
    wgu_b = w_gate_up[l]
    bgu = b_gate_up[l].astype(F32).reshape(N_EXPERTS, 1, 2 * D_FF)
    wdn_b = w_down[l]
    bdn = b_down[l].astype(F32).reshape(N_EXPERTS, 1, D_MODEL)

    meta_pad = jnp.zeros((LANE, D_MODEL), F32).at[:N_META].set(meta_tokens.astype(F32))
    cos_m, sin_m = _rope_tables_t(0.0, LANE)
    _, kd_m, vd_t_m, _, kn_m, vn_t_m, _, _ = _in_proj(meta_pad, g1, w_in_t, w_in_g, cos_m, sin_m,
                                                      gqd, gkd, gqn, gkn, LANE)
    kd_m, kn_m = kd_m[:N_META], kn_m[:N_META]
    vd_t_m, vn_t_m = vd_t_m[:, :N_META], vn_t_m[:, :N_META]

    def trunk(x):
        batch, n, _ = x.shape
        t = batch * n
        x2d = x.reshape(t, D_MODEL)
        tm = _pick(n, (512, 256, 128))
        cos_t, sin_t = _rope_tables_t(float(N_META), n)
        qd_t, kd, vd_t, qn_t, kn, vn_t, ga, gb = _in_proj(x2d, g1, w_in_t, w_in_g, cos_t, sin_t,
                                                         gqd, gkd, gqn, gkn, tm)
        tq = _pick(n, (1024, 512, 256, 128))
        o_a = _diff_attn(diff_bounded, qd_t, kd, vd_t, kd_m, vd_t_m, lq1, lk1, lq2, lk2, subg,
                         batch, n, tq, _pick(n, (512, 256, 128)))
        plan, sigs = _na_plan(n // GRID_W)
        if tuple(sigs) not in bias_cache:
            bias_cache[tuple(sigs)] = _na_bias_tables(na_rpb[l], na_meta_bias[l], sigs)
        bias, mbias = bias_cache[tuple(sigs)]
        o_b = _na_attn(na_bounded, qn_t, kn, vn_t, kn_m, vn_t_m, bias, mbias, plan, batch, n)
        x1, xn, topi, gates = _out_proj(o_a, o_b, ga, gb, x2d, wd_b, wn_b, wo_b, g2, wr, br, tm)
        bm = 256
        pos, block_e, n_used, pad_block_row = _route_plan(topi[:, :TOP_K], t, bm)
        xs = _dispatch(pos, pad_block_row, n_used, xn, block_e.shape[0], tm, bm)
        ys = _moe(n_used, block_e, xs, wgu_b, bgu, wdn_b, bdn, bm)
        y = _combine(pos, x1, gates, ys, LANE)
        return y.reshape(batch, n, D_MODEL)

    bias_cache = {}

    return (trunk(x_prompt), trunk(x_sample))
```
